```python
import jax, jax.numpy as jnp
from jax import lax
import numpy as np

D_MODEL = 4096
BATCH = 2
SEQ = 8192
DEPTH = 2

GRID_W = 64
CTX_LEN = 256
N_MIXERS = 2
N_MLSTM_LAYERS = (DEPTH + N_MIXERS - 1) // N_MIXERS
N_NA_LAYERS = DEPTH // N_MIXERS
N_ADA = 6
NORM_EPS = 1e-6

MLSTM_HEADS = 8
MLSTM_DQK = D_MODEL // 16
MLSTM_DV = D_MODEL // 8
MLSTM_CHUNK = 64
MLSTM_PROJ = 2 * MLSTM_HEADS * MLSTM_DQK + 2 * MLSTM_HEADS * MLSTM_DV + 4 * MLSTM_HEADS
F_BIAS_LO = 3.0
F_BIAS_HI = 6.0
ROPE_THETA = 10000.0

NA_HEAD_DIM = 128
NA_HEADS = D_MODEL // NA_HEAD_DIM
NA_KH_MAX = 8
NA_KW = 16

PEER_HEADS = 8
PEER_KEYS = 128
PEER_EXPERTS = PEER_KEYS * PEER_KEYS
PEER_DK = 256
PEER_TOPK = 16
PEER_BLOCK = 64

kernel_name = "hybrid_mlstm_natten_peer_dit"


def rms_norm(x, gain):
    xf = x.astype(jnp.float32)
    y = xf * lax.rsqrt(jnp.mean(xf * xf, axis=-1, keepdims=True) + NORM_EPS)
    return (y * gain.astype(jnp.float32)).astype(x.dtype)


def modulate(h, shift, scale):
    return h * (1 + scale) + shift


def axial_rope(n, head_dim):
    quarter = head_dim // 4
    inv = ROPE_THETA ** (-jnp.arange(quarter, dtype=jnp.float32) / quarter)
    t = jnp.arange(n, dtype=jnp.int32)
    row = (t // GRID_W).astype(jnp.float32)
    col = (t % GRID_W).astype(jnp.float32)
    ang = jnp.concatenate([row[:, None] * inv, col[:, None] * inv], axis=-1)
    return jnp.cos(ang), jnp.sin(ang)


def apply_rope(x, cos, sin):
    half = x.shape[-1] // 2
    xf = x.astype(jnp.float32)
    x1, x2 = xf[..., :half], xf[..., half:]
    cs, sn = cos[None, :, None, :], sin[None, :, None, :]
    return jnp.concatenate([x1 * cs - x2 * sn, x2 * cs + x1 * sn], axis=-1).astype(x.dtype)


def mlstm_zero_state(b):
    f32 = jnp.float32
    return (jnp.zeros((b, MLSTM_HEADS, MLSTM_DV, MLSTM_DQK), f32),
            jnp.zeros((b, MLSTM_HEADS, MLSTM_DQK), f32),
            jnp.zeros((b, MLSTM_HEADS), f32))


def mlstm_chunked(q, k, v, i_pre, f_pre, state):
    B, H, T, DK = q.shape
    DV = v.shape[-1]
    L = MLSTM_CHUNK
    NC = T // L

    def to_chunks(a):
        a = a.astype(jnp.float32)
        return jnp.moveaxis(a.reshape((B, H, NC, L) + a.shape[3:]), 2, 0)

    log_f = jax.nn.log_sigmoid(f_pre.astype(jnp.float32))
    xs = (to_chunks(q), to_chunks(k), to_chunks(v), to_chunks(i_pre), to_chunks(log_f))
    lower = jnp.tril(jnp.ones((L, L), dtype=bool))

    def step(carry, inp):
        C, n, m = carry
        qc, kc, vc, ic, lfc = inp
        b = jnp.cumsum(lfc, axis=-1)
        d = jnp.where(lower, b[..., :, None] - b[..., None, :] + ic[..., None, :], -jnp.inf)
        g = b + m[..., None]
        mt = jnp.maximum(g, jnp.max(d, axis=-1))
        s = jnp.einsum('bhtd,bhsd->bhts', qc, kc) * jnp.exp(d - mt[..., None])
        inter = jnp.exp(g - mt)
        num = jnp.einsum('bhts,bhsv->bhtv', s, vc) + inter[..., None] * jnp.einsum('bhvd,bhtd->bhtv', C, qc)
        den = jnp.sum(s, axis=-1) + inter * jnp.einsum('bhd,bhtd->bht', n, qc)
        h = num / jnp.maximum(jnp.abs(den), jnp.exp(-mt))[..., None]
        bl = b[..., -1]
        wl = bl[..., None] - b + ic
        m_new = jnp.maximum(bl + m, jnp.max(wl, axis=-1))
        a = jnp.exp(bl + m - m_new)
        w = jnp.exp(wl - m_new[..., None])
        C_new = a[..., None, None] * C + jnp.einsum('bhsv,bhsd->bhvd', vc * w[..., None], kc)
        n_new = a[..., None] * n + jnp.einsum('bhs,bhsd->bhd', w, kc)
        return (C_new, n_new, m_new), h

    state, hs = lax.scan(step, state, xs)
    return jnp.moveaxis(hs, 0, 2).reshape(B, H, T, DV), state


def mlstm_mixer(hl, hc, w_in, gate_b, head_gain, w_out, cos, sin, need_ctx):
    H, DK, DV = MLSTM_HEADS, MLSTM_DQK, MLSTM_DV
    splits = [H * DK, 2 * H * DK, 2 * H * DK + H * DV, 2 * H * DK + 2 * H * DV]

    def project(h):
        B, T = h.shape[:2]
        q, k, v, o, gates = jnp.split(h @ w_in, splits, axis=-1)
        q = q.reshape(B, T, H, DK) * (DK ** -0.5)
        k = k.reshape(B, T, H, DK)
        v = v.reshape(B, T, H, DV)
        o = o.reshape(B, T, H, DV)
        gates = jnp.transpose((gates + gate_b).reshape(B, T, 4, H), (0, 2, 3, 1))
        return q, k, v, o, gates

    ql, kl, vl, ol, gl = project(hl)
    ql, kl = apply_rope(ql, cos, sin), apply_rope(kl, cos, sin)
    qc, kc, vc, oc, gc = project(hc)
    bhtd = lambda a: jnp.swapaxes(a, 1, 2)
    qlb, klb, vlb = bhtd(ql), bhtd(kl), bhtd(vl)
    qcb, kcb, vcb = bhtd(qc), bhtd(kc), bhtd(vc)
    B = hl.shape[0]

    def direction(idx, flip):
        f = (lambda a: jnp.flip(a, axis=2)) if flip else (lambda a: a)
        h_ctx, st = mlstm_chunked(f(qcb), f(kcb), f(vcb), f(gc[:, 2 * idx]), f(gc[:, 2 * idx + 1]),
                                  mlstm_zero_state(B))
        h_lat, _ = mlstm_chunked(f(qlb), f(klb), f(vlb), f(gl[:, 2 * idx]), f(gl[:, 2 * idx + 1]), st)
        return f(h_ctx), f(h_lat)

    hc_f, hl_f = direction(0, False)
    hc_b, hl_b = direction(1, True)

    def finish(hf, hb, o):
        Bt, T = o.shape[:2]
        h = jnp.swapaxes(hf + hb, 1, 2).astype(o.dtype)
        h = rms_norm(h, head_gain) * jax.nn.sigmoid(o)
        return h.reshape(Bt, T, H * DV) @ w_out

    yl = finish(hl_f, hl_b, ol)
    yc = finish(hc_f, hc_b, oc) if need_ctx else None
    return yl, yc


def na_mixer(hl, hc, w_qkv, q_gain, k_gain, rpb, w_out, need_ctx):
    B, N, _ = hl.shape
    rows = N // GRID_W
    kh = min(NA_KH_MAX, rows)
    H, hd = NA_HEADS, NA_HEAD_DIM

    def project(h):
        T = h.shape[1]
        q, k, v = jnp.split(h @ w_qkv, 3, axis=-1)
        q = rms_norm(q.reshape(B, T, H, hd), q_gain) * (hd ** -0.5)
        k = rms_norm(k.reshape(B, T, H, hd), k_gain)
        return q, k, v.reshape(B, T, H, hd)

    ql, kl, vl = project(hl)
    qc, kc, vc = project(hc)
    qg = ql.reshape(B, rows, GRID_W, H, hd)
    kg = kl.reshape(B, rows, GRID_W, H, hd)
    vg = vl.reshape(B, rows, GRID_W, H, hd)

    cols = np.arange(GRID_W)
    col_start = np.clip(cols - NA_KW // 2, 0, GRID_W - NA_KW)
    col_idx = col_start[:, None] + np.arange(NA_KW)[None, :]
    dc = col_idx - cols[:, None] + (NA_KW - 1)
    bias_cols = rpb[:, :, dc]
    n_loc = kh * NA_KW

    def row_block(r):
        rs = jnp.clip(r - kh // 2, 0, rows - kh)
        q_r = lax.dynamic_index_in_dim(qg, r, axis=1, keepdims=False)
        k_w = lax.dynamic_slice_in_dim(kg, rs, kh, axis=1)[:, :, col_idx]
        v_w = lax.dynamic_slice_in_dim(vg, rs, kh, axis=1)[:, :, col_idx]
        dr = rs + jnp.arange(kh, dtype=jnp.int32) - r + (NA_KH_MAX - 1)
        bias = jnp.transpose(jnp.take(bias_cols, dr, axis=1), (0, 2, 1, 3))
        s_loc = jnp.einsum('bchd,bicjhd->bhcij', q_r, k_w) + bias[None]
        s_loc = s_loc.reshape(B, H, GRID_W, n_loc)
        s_ctx = jnp.einsum('bchd,bshd->bhcs', q_r, kc)
        p = jax.nn.softmax(jnp.concatenate([s_loc.astype(jnp.float32), s_ctx.astype(jnp.float32)], axis=-1),
                           axis=-1).astype(vl.dtype)
        p_loc = p[..., :n_loc].reshape(B, H, GRID_W, kh, NA_KW)
        p_ctx = p[..., n_loc:]
        return (jnp.einsum('bhcij,bicjhd->bchd', p_loc, v_w)
                + jnp.einsum('bhcs,bshd->bchd', p_ctx, vc))

    o = lax.map(row_block, jnp.arange(rows, dtype=jnp.int32))
    yl = jnp.moveaxis(o, 0, 1).reshape(B, N, H * hd) @ w_out
    yc = None
    if need_ctx:
        s = jnp.einsum('bqhd,bkhd->bhqk', qc, kc).astype(jnp.float32)
        p = jax.nn.softmax(s, axis=-1).astype(vc.dtype)
        oc = jnp.einsum('bhqk,bkhd->bqhd', p, vc)
        yc = oc.reshape(B, hc.shape[1], H * hd) @ w_out
    return yl, yc


def peer(x, w_query, sub_keys, u, v):
    T, D = x.shape
    half = PEER_DK // 2

    def block(xb):
        q = (xb @ w_query).reshape(-1, PEER_HEADS, PEER_DK)
        s1 = jnp.einsum('thd,kd->thk', q[..., :half], sub_keys[0]).astype(jnp.float32)
        s2 = jnp.einsum('thd,kd->thk', q[..., half:], sub_keys[1]).astype(jnp.float32)
        v1, i1 = lax.top_k(s1, PEER_TOPK)
        v2, i2 = lax.top_k(s2, PEER_TOPK)
        cand = (v1[..., :, None] + v2[..., None, :]).reshape(-1, PEER_HEADS, PEER_TOPK * PEER_TOPK)
        cidx = (i1[..., :, None] * PEER_KEYS + i2[..., None, :]).reshape(-1, PEER_HEADS, PEER_TOPK * PEER_TOPK)
        top_s, pos = lax.top_k(cand, PEER_TOPK)
        eidx = jnp.take_along_axis(cidx, pos, axis=-1)
        g = jax.nn.softmax(top_s, axis=-1)
        a = jax.nn.gelu(jnp.einsum('thed,td->the', u[eidx], xb), approximate=False)
        return jnp.einsum('the,thed->td', (g * a.astype(jnp.float32)).astype(xb.dtype), v[eidx])

    return lax.map(block, x.reshape(T // PEER_BLOCK, PEER_BLOCK, D)).reshape(T, D)


def setup_inputs(seed: int = 0) -> dict:
    key = jax.random.key(seed)
    ks = jax.random.split(key, 24)
    f32 = jnp.float32
    D = D_MODEL

    def normal(k, shape, scale):
        return jax.random.normal(k, shape, f32) * scale

    H = MLSTM_HEADS
    i_bias = normal(ks[9], (N_MLSTM_LAYERS, 2, H), 0.1)
    f_bias = jnp.linspace(F_BIAS_LO, F_BIAS_HI, H, dtype=f32)[None, None, :] + normal(ks[10], (N_MLSTM_LAYERS, 2, H), 0.1)
    mlstm_gate_b = jnp.stack([i_bias[:, 0], f_bias[:, 0], i_bias[:, 1], f_bias[:, 1]], axis=1).reshape(N_MLSTM_LAYERS, 4 * H)
    return {
        'x': normal(ks[0], (BATCH, SEQ, D), 1.0),
        'c': normal(ks[1], (BATCH, D), 1.0),
        'ctx': normal(ks[2], (BATCH, CTX_LEN, D), 1.0),
        'c_ctx': normal(ks[3], (D,), 1.0),
        'ada_w': normal(ks[4], (DEPTH, D, N_ADA * D), D ** -0.5),
        'ada_b': normal(ks[5], (DEPTH, N_ADA * D), 0.02),
        'norm_mix': 1.0 + normal(ks[6], (DEPTH, D), 0.02),
        'norm_ffn': 1.0 + normal(ks[7], (DEPTH, D), 0.02),
        'mlstm_w_in': normal(ks[8], (N_MLSTM_LAYERS, D, MLSTM_PROJ), D ** -0.5),
        'mlstm_gate_b': mlstm_gate_b,
        'mlstm_head_gain': 1.0 + normal(ks[11], (N_MLSTM_LAYERS, H, MLSTM_DV), 0.02),
        'mlstm_w_out': normal(ks[12], (N_MLSTM_LAYERS, H * MLSTM_DV, D), (H * MLSTM_DV) ** -0.5),
        'na_w_qkv': normal(ks[13], (N_NA_LAYERS, D, 3 * D), D ** -0.5),
        'na_q_gain': 1.0 + normal(ks[14], (N_NA_LAYERS, NA_HEAD_DIM), 0.02),
        'na_k_gain': 1.0 + normal(ks[15], (N_NA_LAYERS, NA_HEAD_DIM), 0.02),
        'na_rpb': normal(ks[16], (N_NA_LAYERS, NA_HEADS, 2 * NA_KH_MAX - 1, 2 * NA_KW - 1), 0.5),
        'na_w_out': normal(ks[17], (N_NA_LAYERS, D, D), D ** -0.5),
        'peer_w_query': normal(ks[18], (DEPTH, D, PEER_HEADS * PEER_DK), D ** -0.5),
        'peer_sub_keys': normal(ks[19], (DEPTH, 2, PEER_KEYS, PEER_DK // 2), (PEER_DK // 2) ** -0.5),
        'peer_u': normal(ks[20], (DEPTH, PEER_EXPERTS, D), D ** -0.5),
        'peer_v': normal(ks[21], (DEPTH, PEER_EXPERTS, D), PEER_HEADS ** -0.5),
    }


def reference(x, c, ctx, c_ctx, ada_w, ada_b, norm_mix, norm_ffn, mlstm_w_in, mlstm_gate_b,
              mlstm_head_gain, mlstm_w_out, na_w_qkv, na_q_gain, na_k_gain, na_rpb, na_w_out,
              peer_w_query, peer_sub_keys, peer_u, peer_v):
    B, N, D = x.shape
    n_ctx = ctx.shape[1]
    cos, sin = axial_rope(N, MLSTM_DQK)
    silu_l = jax.nn.silu(c)[:, None, :]
    silu_c = jax.nn.silu(c_ctx)[None, None, :]
    xl, xc = x, ctx
    for i in range(DEPTH):
        last = i == DEPTH - 1
        j = i // N_MIXERS
        mod_l = jnp.split(silu_l @ ada_w[i] + ada_b[i], N_ADA, axis=-1)
        mod_c = jnp.split(silu_c @ ada_w[i] + ada_b[i], N_ADA, axis=-1)
        hl = modulate(rms_norm(xl, norm_mix[i]), mod_l[0], mod_l[1])
        hc = modulate(rms_norm(xc, norm_mix[i]), mod_c[0], mod_c[1])
        if i % N_MIXERS == 0:
            yl, yc = mlstm_mixer(hl, hc, mlstm_w_in[j], mlstm_gate_b[j], mlstm_head_gain[j],
                                 mlstm_w_out[j], cos, sin, not last)
        else:
            yl, yc = na_mixer(hl, hc, na_w_qkv[j], na_q_gain[j], na_k_gain[j], na_rpb[j],
                              na_w_out[j], not last)
        xl = xl + mod_l[2] * yl
        hl = modulate(rms_norm(xl, norm_ffn[i]), mod_l[3], mod_l[4]).reshape(B * N, D)
        if last:
            y = peer(hl, peer_w_query[i], peer_sub_keys[i], peer_u[i], peer_v[i])
            xl = xl + mod_l[5] * y.reshape(B, N, D)
        else:
            xc = xc + mod_c[2] * yc
            hc = modulate(rms_norm(xc, norm_ffn[i]), mod_c[3], mod_c[4]).reshape(B * n_ctx, D)
            y = peer(jnp.concatenate([hl, hc], axis=0), peer_w_query[i], peer_sub_keys[i], peer_u[i], peer_v[i])
            xl = xl + mod_l[5] * y[:B * N].reshape(B, N, D)
            xc = xc + mod_c[5] * y[B * N:].reshape(B, n_ctx, D)
    return xl
```

```python
import functools

import numpy as np
import jax
import jax.numpy as jnp
from jax import lax
from jax.experimental import pallas as pl
from jax.experimental.pallas import tpu as pltpu

F32 = jnp.float32
BF16 = jnp.bfloat16

NORM_EPS = 1e-6
GRID_W = 64
N_ADA = 6
MLSTM_HEADS = 8
ROPE_THETA = 10000.0
NA_HEAD_DIM = 128
NA_KH = 8
NA_KW = 16
PEER_HEADS = 8
PEER_TOPK = 16

LANES = 128
SUBLANES = 8
CHUNK = 256
NA_QROWS = CHUNK // GRID_W
ROW_TILE = 256
MM_TM = 512
MM_TN = 1024
ADA_TN = 512
PEER_TM = 512
PEER_TE = 512
SEL_TM = 256
VMEM_LIMIT = 56 * 1024 * 1024


def _params(sem):
    return pltpu.CompilerParams(dimension_semantics=sem, vmem_limit_bytes=VMEM_LIMIT)


def _mod_row_map(tiles_per_batch, n_batch):
    return lambda i, *_: (jnp.minimum(i // tiles_per_batch, n_batch), 0, 0)


def _ada_kernel(c_ref, w_ref, b_ref, o_ref):
    c = c_ref[...]
    s = (c * jax.nn.sigmoid(c)).astype(BF16)
    o_ref[...] = jnp.dot(s, w_ref[...].astype(BF16), preferred_element_type=F32) + b_ref[...]


def _ada(cvec, ada_w, ada_b):
    depth, d, n6 = ada_w.shape
    rows = cvec.shape[0]
    return pl.pallas_call(
        _ada_kernel,
        grid=(depth, n6 // ADA_TN),
        in_specs=[
            pl.BlockSpec((rows, d), lambda l, j: (0, 0)),
            pl.BlockSpec((None, d, ADA_TN), lambda l, j: (l, 0, j)),
            pl.BlockSpec((None, 1, ADA_TN), lambda l, j: (l, 0, j)),
        ],
        out_specs=pl.BlockSpec((None, rows, ADA_TN), lambda l, j: (l, 0, j)),
        out_shape=jax.ShapeDtypeStruct((depth, rows, n6), F32),
        compiler_params=_params(("arbitrary", "arbitrary")),
        name="ada",
    )(cvec, ada_w, ada_b.reshape(depth, 1, n6))


def _normmod_kernel(x_ref, g_ref, sh_ref, sc_ref, o_ref, *, transpose):
    x = x_ref[...]
    y = x * lax.rsqrt(jnp.mean(x * x, axis=-1, keepdims=True) + NORM_EPS) * g_ref[...]
    h = y * (1.0 + sc_ref[...]) + sh_ref[...]
    o_ref[...] = (h.T if transpose else h).astype(BF16)


def _normmod(x, t_rows, gain, shift, scale, tiles_per_batch, n_batch, transpose):
    d = x.shape[1]
    tm = ROW_TILE
    row_map = _mod_row_map(tiles_per_batch, n_batch)
    if transpose:
        out_spec = pl.BlockSpec((d, tm), lambda i: (0, i))
        out_shape = jax.ShapeDtypeStruct((d, t_rows), BF16)
    else:
        out_spec = pl.BlockSpec((tm, d), lambda i: (i, 0))
        out_shape = jax.ShapeDtypeStruct((t_rows, d), BF16)
    return pl.pallas_call(
        functools.partial(_normmod_kernel, transpose=transpose),
        grid=(t_rows // tm,),
        in_specs=[
            pl.BlockSpec((tm, d), lambda i: (i, 0)),
            pl.BlockSpec((1, d), lambda i: (0, 0)),
            pl.BlockSpec((None, 1, d), row_map),
            pl.BlockSpec((None, 1, d), row_map),
        ],
        out_specs=out_spec,
        out_shape=out_shape,
        compiler_params=_params(("arbitrary",)),
        name="normmod_t" if transpose else "normmod",
    )(x, gain.reshape(1, d), shift, scale)


def _mm_kernel(a_ref, w_ref, o_ref):
    o_ref[...] = jnp.dot(a_ref[...], w_ref[...], preferred_element_type=F32).astype(o_ref.dtype)


def _mm_resid_kernel(a_ref, w_ref, r_ref, g_ref, o_ref):
    acc = jnp.dot(a_ref[...], w_ref[...], preferred_element_type=F32)
    o_ref[...] = r_ref[...] + g_ref[...] * acc


def _matmul(a, w, out_dtype, tn=MM_TN):
    m, k = a.shape
    n = w.shape[1]
    tn = min(tn, n)
    return pl.pallas_call(
        _mm_kernel,
        grid=(m // MM_TM, n // tn),
        in_specs=[
            pl.BlockSpec((MM_TM, k), lambda i, j: (i, 0)),
            pl.BlockSpec((k, tn), lambda i, j: (0, j)),
        ],
        out_specs=pl.BlockSpec((MM_TM, tn), lambda i, j: (i, j)),
        out_shape=jax.ShapeDtypeStruct((m, n), out_dtype),
        compiler_params=_params(("arbitrary", "arbitrary")),
        name="matmul",
    )(a, w)


def _matmul_resid(a, w, resid, gate, tiles_per_batch, n_batch):
    m, k = a.shape
    n = w.shape[1]
    tn = min(MM_TN, n)
    row_map = _mod_row_map(tiles_per_batch, n_batch)
    return pl.pallas_call(
        _mm_resid_kernel,
        grid=(m // MM_TM, n // tn),
        in_specs=[
            pl.BlockSpec((MM_TM, k), lambda i, j: (i, 0)),
            pl.BlockSpec((k, tn), lambda i, j: (0, j)),
            pl.BlockSpec((MM_TM, tn), lambda i, j: (i, j)),
            pl.BlockSpec((None, 1, tn), lambda i, j: row_map(i)[:2] + (j,)),
        ],
        out_specs=pl.BlockSpec((MM_TM, tn), lambda i, j: (i, j)),
        out_shape=jax.ShapeDtypeStruct((m, n), F32),
        compiler_params=_params(("arbitrary", "arbitrary")),
        name="matmul_resid",
    )(a, w, resid, gate)


def _log_sigmoid(x):
    return jnp.minimum(x, 0.0) - jnp.log1p(jnp.exp(-jnp.abs(x)))


def _split_dot(lhs, rhs, split_lhs):
    r = lhs if split_lhs else rhs
    acc = None
    for _ in range(3):
        p = r.astype(BF16)
        t = (jnp.dot(p, rhs, preferred_element_type=F32) if split_lhs
             else jnp.dot(lhs, p, preferred_element_type=F32))
        acc = t if acc is None else acc + t
        r = r - p.astype(F32)
    return acc


def _mlstm_kernel(*refs, reverse, finish, nctx, heads, dk, dv):
    if finish:
        (q_ref, k_ref, v_ref, gc_ref, gr_ref, bc_ref, br_ref, cos_ref, sin_ref,
         hf_ref, og_ref, hg_ref, out_ref, ct_ref, m_ref) = refs
    else:
        (q_ref, k_ref, v_ref, gc_ref, gr_ref, bc_ref, br_ref, cos_ref, sin_ref,
         out_ref, ct_ref, m_ref) = refs
    h = pl.program_id(1)
    c = pl.program_id(2)
    L = CHUNK

    @pl.when(c == 0)
    def _():
        ct_ref[...] = jnp.zeros_like(ct_ref)
        m_ref[...] = jnp.zeros_like(m_ref)

    is_lat = c >= nctx
    cs = jnp.where(is_lat, cos_ref[...], 1.0)
    sn = jnp.where(is_lat, sin_ref[...], 0.0)
    half = dk // 2

    def rope(x):
        x1, x2 = x[:, :half], x[:, half:]
        return jnp.concatenate([x1 * cs - x2 * sn, x2 * cs + x1 * sn], axis=-1)

    qb = (rope(q_ref[...].astype(F32)) * (dk ** -0.5)).astype(BF16)
    kb = rope(k_ref[...].astype(F32)).astype(BF16)

    direction = 1 if reverse else 0
    idx_i = (2 * direction) * heads + h
    idx_f = (2 * direction + 1) * heads + h
    lane = lax.broadcasted_iota(jnp.int32, (1, LANES), 1)
    gc = gc_ref[...] + bc_ref[...]
    i_col = jnp.sum(jnp.where(lane == idx_i, gc, 0.0), axis=1, keepdims=True)
    f_col = jnp.sum(jnp.where(lane == idx_f, gc, 0.0), axis=1, keepdims=True)
    sub = lax.broadcasted_iota(jnp.int32, (4 * heads, 1), 0)
    gr = gr_ref[...] + br_ref[...]
    i_row = jnp.sum(jnp.where(sub == idx_i, gr, 0.0), axis=0, keepdims=True)
    f_row = jnp.sum(jnp.where(sub == idx_f, gr, 0.0), axis=0, keepdims=True)
    lf_col = _log_sigmoid(f_col)
    lf_row = _log_sigmoid(f_row)

    ti = lax.broadcasted_iota(jnp.int32, (L, L), 0)
    si = lax.broadcasted_iota(jnp.int32, (L, L), 1)
    mask = (si >= ti) if reverse else (si <= ti)
    mask_t = (ti >= si) if reverse else (ti <= si)
    tri = jnp.where(mask, 1.0, 0.0).astype(BF16)
    tri_t = jnp.where(mask_t, 1.0, 0.0).astype(BF16)
    b_col = _split_dot(tri, jnp.broadcast_to(lf_col, (L, LANES)), split_lhs=False)[:, 0:1]
    b_row = _split_dot(jnp.broadcast_to(lf_row, (2 * SUBLANES, L)), tri_t, split_lhs=True)[0:1]

    d = jnp.where(mask, b_col - b_row + i_row, -jnp.inf)
    m_prev = m_ref[0:1, 0:1]
    g = b_col + m_prev
    mt = jnp.maximum(g, jnp.max(d, axis=1, keepdims=True))
    s = lax.dot_general(qb, kb, (((1,), (1,)), ((), ())), preferred_element_type=F32)
    s = s * jnp.exp(d - mt)
    inter = jnp.exp(g - mt)

    v_aug = jnp.concatenate([v_ref[...], jnp.ones((L, LANES), BF16)], axis=1)
    nd = (jnp.dot(s.astype(BF16), v_aug, preferred_element_type=F32)
          + inter * jnp.dot(qb, ct_ref[...].astype(BF16), preferred_element_type=F32))
    den = nd[:, dv:dv + 1]
    hout = nd[:, :dv] * (1.0 / jnp.maximum(jnp.abs(den), jnp.exp(-mt)))

    bl = b_col[0:1] if reverse else b_col[L - 1:L]
    wl = bl - b_col + i_col
    m_new = jnp.maximum(bl + m_prev, jnp.max(wl, axis=0, keepdims=True))
    a = jnp.exp(bl + m_prev - m_new)
    w = jnp.exp(wl - m_new)
    vw = (v_aug.astype(F32) * w).astype(BF16)
    ct_ref[...] = a * ct_ref[...] + lax.dot_general(
        kb, vw, (((0,), (0,)), ((), ())), preferred_element_type=F32)
    m_ref[...] = jnp.broadcast_to(m_new, m_ref.shape)

    if finish:
        hs = hf_ref[...] + hout
        y = hs * lax.rsqrt(jnp.mean(hs * hs, axis=-1, keepdims=True) + NORM_EPS) * hg_ref[...]
        out_ref[...] = (y * jax.nn.sigmoid(og_ref[...].astype(F32))).astype(out_ref.dtype)
    else:
        out_ref[...] = hout


def _mlstm_pass(proj, gates_c, gates_r, bias_c, bias_r, cos, sin, n_batch, n_lat, n_ctx,
                reverse, h_fwd=None, head_gain=None):
    t_all = proj.shape[0]
    heads = MLSTM_HEADS
    d = proj.shape[1] // 3
    dk, dv = d // 16, d // 8
    nlat, nctx = n_lat // CHUNK, n_ctx // CHUNK
    finish = h_fwd is not None

    def blk(b, c):
        cc = (nctx - 1 - c) if reverse else c
        lc = (nlat - 1 - (c - nctx)) if reverse else (c - nctx)
        return jnp.where(c < nctx, n_batch * nlat + b * nctx + cc, b * nlat + lc)

    def lat_blk(c):
        lc = (nlat - 1 - (c - nctx)) if reverse else (c - nctx)
        return jnp.clip(lc, 0, nlat - 1)

    in_specs = [
        pl.BlockSpec((CHUNK, dk), lambda b, h, c: (blk(b, c), h)),
        pl.BlockSpec((CHUNK, dk), lambda b, h, c: (blk(b, c), heads + h)),
        pl.BlockSpec((CHUNK, dv), lambda b, h, c: (blk(b, c), heads + h)),
        pl.BlockSpec((CHUNK, LANES), lambda b, h, c: (blk(b, c), 0)),
        pl.BlockSpec((4 * heads, CHUNK), lambda b, h, c: (0, blk(b, c))),
        pl.BlockSpec((1, LANES), lambda b, h, c: (0, 0)),
        pl.BlockSpec((4 * heads, 1), lambda b, h, c: (0, 0)),
        pl.BlockSpec((CHUNK, dk // 2), lambda b, h, c: (lat_blk(c), 0)),
        pl.BlockSpec((CHUNK, dk // 2), lambda b, h, c: (lat_blk(c), 0)),
    ]
    args = [proj, proj, proj, gates_c, gates_r, bias_c, bias_r, cos, sin]
    if finish:
        in_specs += [
            pl.BlockSpec((CHUNK, dv), lambda b, h, c: (blk(b, c), h)),
            pl.BlockSpec((CHUNK, dv), lambda b, h, c: (blk(b, c), 2 * heads + h)),
            pl.BlockSpec((None, 1, dv), lambda b, h, c: (h, 0, 0)),
        ]
        args += [h_fwd, proj, head_gain.reshape(heads, 1, dv)]
    return pl.pallas_call(
        functools.partial(_mlstm_kernel, reverse=reverse, finish=finish, nctx=nctx,
                          heads=heads, dk=dk, dv=dv),
        grid=(n_batch, heads, nctx + nlat),
        in_specs=in_specs,
        out_specs=pl.BlockSpec((CHUNK, dv), lambda b, h, c: (blk(b, c), h)),
        out_shape=jax.ShapeDtypeStruct((t_all, heads * dv), BF16 if finish else F32),
        scratch_shapes=[pltpu.VMEM((dk, dv + LANES), F32), pltpu.VMEM((SUBLANES, LANES), F32)],
        compiler_params=_params(("arbitrary", "arbitrary", "arbitrary")),
        name="mlstm_bwd_finish" if finish else "mlstm_fwd",
    )(*args)


def _na_bias_blocks(rpb, rows):
    qr, span = NA_QROWS, 3 * NA_QROWS
    rl = np.arange(qr)[:, None, None, None]
    c = np.arange(GRID_W)[None, :, None, None]
    i = np.arange(span)[None, None, :, None]
    cp = np.arange(GRID_W)[None, None, None, :]
    shape = (qr, GRID_W, span, GRID_W)
    out = []
    for r0, u0 in ((0, 0), (qr, 0), (rows - qr, rows - span)):
        r, ia = r0 + rl, u0 + i
        rs = np.clip(r - NA_KH // 2, 0, rows - NA_KH)
        cs = np.clip(c - NA_KW // 2, 0, GRID_W - NA_KW)
        valid = (ia >= rs) & (ia < rs + NA_KH) & (cp >= cs) & (cp < cs + NA_KW)
        dr = np.clip(ia - r + NA_KH - 1, 0, 2 * NA_KH - 2)
        dc = np.clip(cp - c + NA_KW - 1, 0, 2 * NA_KW - 2)
        valid = np.broadcast_to(valid, shape).reshape(qr * GRID_W, span * GRID_W)
        dr = np.broadcast_to(dr, shape).reshape(qr * GRID_W, span * GRID_W)
        dc = np.broadcast_to(dc, shape).reshape(qr * GRID_W, span * GRID_W)
        out.append(jnp.where(valid[None], rpb[:, dr, dc], -jnp.inf))
    return jnp.stack(out, axis=0)


def _na_kernel(q_ref, k0_ref, k1_ref, k2_ref, v0_ref, v1_ref, v2_ref, kc_ref, vc_ref,
               bias_ref, qg_ref, kg_ref, o_ref):
    hd = q_ref.shape[-1]

    def nrm(ref, gain_ref):
        x = ref[...].astype(F32)
        return x * lax.rsqrt(jnp.mean(x * x, axis=-1, keepdims=True) + NORM_EPS) * gain_ref[...]

    nt = (((1,), (1,)), ((), ()))
    q = (nrm(q_ref, qg_ref) * (hd ** -0.5)).astype(BF16)
    scores = []
    for j, k_ref in enumerate((k0_ref, k1_ref, k2_ref)):
        k = nrm(k_ref, kg_ref).astype(BF16)
        scores.append(lax.dot_general(q, k, nt, preferred_element_type=F32)
                      + bias_ref[:, j * CHUNK:(j + 1) * CHUNK])
    scores.append(lax.dot_general(q, nrm(kc_ref, kg_ref).astype(BF16), nt, preferred_element_type=F32))
    m = scores[0].max(axis=-1, keepdims=True)
    for s in scores[1:]:
        m = jnp.maximum(m, s.max(axis=-1, keepdims=True))
    acc = None
    l = None
    for s, v_ref in zip(scores, (v0_ref, v1_ref, v2_ref, vc_ref)):
        p = jnp.exp(s - m)
        ls = p.sum(axis=-1, keepdims=True)
        pv = jnp.dot(p.astype(BF16), v_ref[...], preferred_element_type=F32)
        l = ls if l is None else l + ls
        acc = pv if acc is None else acc + pv
    o_ref[...] = (acc * (1.0 / l)).astype(o_ref.dtype)


def _na_attention(qkv, bias, q_gain, k_gain, n_batch, n_lat, n_ctx):
    d = qkv.shape[1] // 3
    hd = NA_HEAD_DIM
    nh = d // hd
    nrb = n_lat // CHUNK
    assert n_ctx == CHUNK and nrb >= 3

    def kv_map(off, j):
        return lambda h, b, r: (b * nrb + jnp.clip(r - 1, 0, nrb - 3) + j, off * nh + h)

    def variant(r):
        return jnp.where(r == 0, 0, jnp.where(r == nrb - 1, 2, 1))

    blk = (CHUNK, hd)
    in_specs = [pl.BlockSpec(blk, lambda h, b, r: (b * nrb + r, h))]
    in_specs += [pl.BlockSpec(blk, kv_map(1, j)) for j in range(3)]
    in_specs += [pl.BlockSpec(blk, kv_map(2, j)) for j in range(3)]
    in_specs += [
        pl.BlockSpec(blk, lambda h, b, r: (n_batch * nrb + b, nh + h)),
        pl.BlockSpec(blk, lambda h, b, r: (n_batch * nrb + b, 2 * nh + h)),
        pl.BlockSpec((None, None, CHUNK, 3 * CHUNK), lambda h, b, r: (variant(r), h, 0, 0)),
        pl.BlockSpec((1, hd), lambda h, b, r: (0, 0)),
        pl.BlockSpec((1, hd), lambda h, b, r: (0, 0)),
    ]
    return pl.pallas_call(
        _na_kernel,
        grid=(nh, n_batch, nrb),
        in_specs=in_specs,
        out_specs=pl.BlockSpec(blk, lambda h, b, r: (b * nrb + r, h)),
        out_shape=jax.ShapeDtypeStruct((n_batch * n_lat, d), BF16),
        compiler_params=_params(("arbitrary", "arbitrary", "arbitrary")),
        name="na_attention",
    )(*([qkv] * 9), bias, q_gain.reshape(1, hd), k_gain.reshape(1, hd))


def _top16_rows(x):
    kk, n = x.shape
    kio = lax.broadcasted_iota(jnp.int32, (kk, n), 0)
    rio = lax.broadcasted_iota(jnp.int32, (PEER_TOPK, n), 0)
    out = jnp.zeros((PEER_TOPK, n), F32)
    for r in range(PEER_TOPK):
        m = jnp.max(x, axis=0, keepdims=True)
        out = jnp.where(rio == r, m, out)
        first = jnp.min(jnp.where(x == m, kio, kk), axis=0, keepdims=True)
        x = jnp.where(kio == first, -jnp.inf, x)
    return out


def _peer_select_kernel(ht_ref, wq_ref, k1_ref, k2_ref, s1_ref, s2_ref, st_ref):
    qt = jnp.dot(wq_ref[...], ht_ref[...], preferred_element_type=F32)
    half = qt.shape[0] // 2
    s1 = jnp.dot(k1_ref[...], qt[:half].astype(BF16), preferred_element_type=F32)
    s2 = jnp.dot(k2_ref[...], qt[half:].astype(BF16), preferred_element_type=F32)
    r8 = lax.broadcasted_iota(jnp.int32, (SUBLANES, LANES), 0)
    for lb in range(s1.shape[1] // LANES):
        sl = slice(lb * LANES, (lb + 1) * LANES)
        x1, x2 = s1[:, sl], s2[:, sl]
        a, b = _top16_rows(x1), _top16_rows(x2)
        a0, b0 = a[0:1], b[0:1]
        a, b = a - a0, b - b0
        s1_ref[:, sl] = x1 - a0
        s2_ref[:, sl] = x2 - b0
        b8 = b[0:SUBLANES]
        blocks = [a[0:1] + b8, a[0:1] + b[SUBLANES:], a[1:2] + b8]
        for i in range(2, SUBLANES):
            blocks.append(jnp.where(r8 < PEER_TOPK // (i + 1), a[i:i + 1] + b8, -jnp.inf))
        blocks.append(a[SUBLANES:] + b[0:1])
        tau = jnp.full((1, LANES), -jnp.inf, F32)
        for i in range(PEER_TOPK):
            for j in range(PEER_TOPK // (i + 1)):
                cand = a[i:i + 1] + b[j:j + 1]
                cnt = jnp.zeros((SUBLANES, LANES), F32)
                for blkv in blocks:
                    cnt = cnt + jnp.where(blkv >= cand, 1.0, 0.0)
                cnt = jnp.sum(cnt, axis=0, keepdims=True)
                tau = jnp.maximum(tau, jnp.where(cnt >= PEER_TOPK, cand, -jnp.inf))
        z = jnp.zeros((SUBLANES, LANES), F32)
        for blkv in blocks:
            z = z + jnp.where(blkv >= tau, jnp.exp(blkv), 0.0)
        rz = 1.0 / jnp.sum(z, axis=0, keepdims=True)
        st_ref[:, sl] = jnp.where(r8 == 0, tau, jnp.where(r8 == 1, rz, 0.0))


def _peer_select(ht, wq_t, keys1, keys2):
    d, t = ht.shape
    heads = PEER_HEADS
    dk = wq_t.shape[0] // heads
    nk = keys1.shape[0]
    tm = SEL_TM
    out_s = jax.ShapeDtypeStruct((heads, nk, t), F32)
    return pl.pallas_call(
        _peer_select_kernel,
        grid=(t // tm, heads),
        in_specs=[
            pl.BlockSpec((d, tm), lambda i, h: (0, i)),
            pl.BlockSpec((dk, d), lambda i, h: (h, 0)),
            pl.BlockSpec((nk, dk // 2), lambda i, h: (0, 0)),
            pl.BlockSpec((nk, dk // 2), lambda i, h: (0, 0)),
        ],
        out_specs=[
            pl.BlockSpec((None, nk, tm), lambda i, h: (h, 0, i)),
            pl.BlockSpec((None, nk, tm), lambda i, h: (h, 0, i)),
            pl.BlockSpec((None, SUBLANES, tm), lambda i, h: (h, 0, i)),
        ],
        out_shape=[out_s, out_s, jax.ShapeDtypeStruct((heads, SUBLANES, t), F32)],
        compiler_params=_params(("arbitrary", "arbitrary")),
        name="peer_select",
    )(ht, wq_t, keys1, keys2)


def _gelu(x):
    return 0.5 * x * (1.0 + lax.erf(x * np.float32(1.0 / np.sqrt(2.0))))


def _peer_dense_kernel(ht_ref, u_ref, vt_ref, s1_ref, s2_ref, st_ref, o_ref, hid_ref, act_ref):
    j = pl.program_id(1)
    heads, nk, tm = s2_ref.shape
    te = u_ref.shape[0]

    @pl.when(j == 0)
    def _():
        o_ref[...] = jnp.zeros_like(o_ref)

    hid_ref[...] = jnp.dot(u_ref[...], ht_ref[...], preferred_element_type=F32)

    def group(kk, carry):
        k1 = j * (te // nk) + kk
        row0 = pl.multiple_of(kk * nk, nk)
        k1_group = pl.multiple_of((k1 // SUBLANES) * SUBLANES, SUBLANES)
        k1_pick = lax.broadcasted_iota(jnp.int32, (SUBLANES, LANES), 0) == k1 % SUBLANES
        for lb in range(tm // LANES):
            sl = slice(lb * LANES, (lb + 1) * LANES)
            gate = jnp.zeros((nk, LANES), F32)
            for h in range(heads):
                s1_rows = s1_ref[h, pl.ds(k1_group, SUBLANES), sl]
                s1_row = jnp.sum(jnp.where(k1_pick, s1_rows, 0.0), axis=0, keepdims=True)
                s = s1_row + s2_ref[h, :, sl]
                wgt = jnp.exp(s) * st_ref[h, 1:2, sl]
                gate = gate + jnp.where(s >= st_ref[h, 0:1, sl], wgt, 0.0)
            hid = hid_ref[pl.ds(row0, nk), sl]
            act_ref[pl.ds(row0, nk), sl] = (gate * _gelu(hid)).astype(BF16)
        return carry

    lax.fori_loop(0, te // nk, group, 0)
    o_ref[...] += jnp.dot(vt_ref[...], act_ref[...], preferred_element_type=F32)


def _peer_dense(ht, u, vt, s1, s2, stats):
    d, t = ht.shape
    n_exp = u.shape[0]
    heads, nk, _ = s1.shape
    tm, te = PEER_TM, PEER_TE
    return pl.pallas_call(
        _peer_dense_kernel,
        grid=(t // tm, n_exp // te),
        in_specs=[
            pl.BlockSpec((d, tm), lambda i, j: (0, i)),
            pl.BlockSpec((te, d), lambda i, j: (j, 0)),
            pl.BlockSpec((d, te), lambda i, j: (0, j)),
            pl.BlockSpec((heads, nk, tm), lambda i, j: (0, 0, i)),
            pl.BlockSpec((heads, nk, tm), lambda i, j: (0, 0, i)),
            pl.BlockSpec((heads, SUBLANES, tm), lambda i, j: (0, 0, i)),
        ],
        out_specs=pl.BlockSpec((d, tm), lambda i, j: (0, i)),
        out_shape=jax.ShapeDtypeStruct((d, t), F32),
        scratch_shapes=[pltpu.VMEM((te, tm), F32), pltpu.VMEM((te, tm), BF16)],
        compiler_params=_params(("arbitrary", "arbitrary")),
        name="peer_dense",
    )(ht, u, vt, s1, s2, stats)


def _resid_t_kernel(x_ref, yt_ref, g_ref, o_ref):
    o_ref[...] = x_ref[...] + g_ref[...] * yt_ref[...].T


def _resid_t(x, yt, gate, tiles_per_batch, n_batch):
    d, t = yt.shape
    tm = ROW_TILE
    return pl.pallas_call(
        _resid_t_kernel,
        grid=(t // tm,),
        in_specs=[
            pl.BlockSpec((tm, d), lambda i: (i, 0)),
            pl.BlockSpec((d, tm), lambda i: (0, i)),
            pl.BlockSpec((None, 1, d), _mod_row_map(tiles_per_batch, n_batch)),
        ],
        out_specs=pl.BlockSpec((tm, d), lambda i: (i, 0)),
        out_shape=jax.ShapeDtypeStruct((t, d), F32),
        compiler_params=_params(("arbitrary",)),
        name="resid_t",
    )(x, yt, gate)


def _peer_ffn(x, t_rows, gain, mods, w_query, sub_keys, u, v, n_lat, n_batch):
    shift, scale, gate = mods
    ht = _normmod(x, t_rows, gain, shift, scale, n_lat // ROW_TILE, n_batch, transpose=True)
    s1, s2, stats = _peer_select(ht, w_query.T.astype(BF16), sub_keys[0].astype(BF16),
                                 sub_keys[1].astype(BF16))
    yt = _peer_dense(ht, u.astype(BF16), v.T.astype(BF16), s1, s2, stats)
    return _resid_t(x, yt, gate, n_lat // ROW_TILE, n_batch)


def _axial_rope(n, head_dim):
    quarter = head_dim // 4
    inv = ROPE_THETA ** (-jnp.arange(quarter, dtype=F32) / quarter)
    t = jnp.arange(n, dtype=jnp.int32)
    row = (t // GRID_W).astype(F32)
    col = (t % GRID_W).astype(F32)
    ang = jnp.concatenate([row[:, None] * inv, col[:, None] * inv], axis=-1)
    return jnp.cos(ang), jnp.sin(ang)


def kernel(x, c, ctx, c_ctx, ada_w, ada_b, norm_mix, norm_ffn, mlstm_w_in, mlstm_gate_b, mlstm_head_gain, mlstm_w_out, na_w_qkv, na_q_gain, na_k_gain, na_rpb, na_w_out, peer_w_query, peer_sub_keys, peer_u, peer_v):
    n_batch, n_lat, d = x.shape
    n_ctx = ctx.shape[1]
    depth = ada_w.shape[0]
    t_lat = n_batch * n_lat
    t_all = t_lat + n_batch * n_ctx
    heads = MLSTM_HEADS
    assert depth == 2 and n_lat % MM_TM == 0 and t_all % MM_TM == 0 and n_ctx % CHUNK == 0

    mod_rows = 2 * SUBLANES
    cvec = jnp.concatenate(
        [c, c_ctx[None, :], jnp.zeros((mod_rows - n_batch - 1, d), F32)], axis=0)
    mod = _ada(cvec, ada_w, ada_b).reshape(depth, mod_rows, N_ADA, d)

    def mods(layer, k):
        return mod[layer, :, k, :].reshape(mod_rows, 1, d)

    xs = jnp.concatenate([x.reshape(t_lat, d), ctx.reshape(n_batch * n_ctx, d)], axis=0)

    hmix = _normmod(xs, t_all, norm_mix[0], mods(0, 0), mods(0, 1), n_lat // ROW_TILE, n_batch,
                    transpose=False)
    w_in = mlstm_w_in[0]
    proj = _matmul(hmix, w_in[:, :3 * d].astype(BF16), BF16)
    w_gate = jnp.pad(w_in[:, 3 * d:], ((0, 0), (0, LANES - 4 * heads))).astype(BF16)
    gates_c = _matmul(hmix, w_gate, F32)
    gates_r = gates_c[:, :4 * heads].T
    bias_c = jnp.pad(mlstm_gate_b[0], (0, LANES - 4 * heads)).reshape(1, LANES)
    bias_r = mlstm_gate_b[0].reshape(4 * heads, 1)
    cos, sin = _axial_rope(n_lat, d // 16)
    scan_args = (proj, gates_c, gates_r, bias_c, bias_r, cos, sin, n_batch, n_lat, n_ctx)
    h_fwd = _mlstm_pass(*scan_args, reverse=False)
    ymix = _mlstm_pass(*scan_args, reverse=True, h_fwd=h_fwd, head_gain=mlstm_head_gain[0])
    xs = _matmul_resid(ymix, mlstm_w_out[0].astype(BF16), xs, mods(0, 2), n_lat // MM_TM, n_batch)
    xs = _peer_ffn(xs, t_all, norm_ffn[0], (mods(0, 3), mods(0, 4), mods(0, 5)),
                   peer_w_query[0], peer_sub_keys[0], peer_u[0], peer_v[0], n_lat, n_batch)

    hmix = _normmod(xs, t_all, norm_mix[1], mods(1, 0), mods(1, 1), n_lat // ROW_TILE, n_batch,
                    transpose=False)
    qkv = _matmul(hmix, na_w_qkv[0].astype(BF16), BF16)
    bias = _na_bias_blocks(na_rpb[0], n_lat // GRID_W)
    att = _na_attention(qkv, bias, na_q_gain[0], na_k_gain[0], n_batch, n_lat, n_ctx)
    xs = _matmul_resid(att, na_w_out[0].astype(BF16), xs, mods(1, 2), n_lat // MM_TM, n_batch)
    xs = _peer_ffn(xs, t_lat, norm_ffn[1], (mods(1, 3), mods(1, 4), mods(1, 5)),
                   peer_w_query[1], peer_sub_keys[1], peer_u[1], peer_v[1], n_lat, n_batch)
    return xs.reshape(n_batch, n_lat, d)
```

```python
import functools

import numpy as np
import jax
import jax.numpy as jnp
from jax import lax
from jax.experimental import pallas as pl
from jax.experimental.pallas import tpu as pltpu

F32 = jnp.float32
BF16 = jnp.bfloat16

NORM_EPS = 1e-6
GRID_W = 64
N_ADA = 6
MLSTM_HEADS = 8
ROPE_THETA = 10000.0
NA_HEAD_DIM = 128
NA_KH = 8
NA_KW = 16
PEER_HEADS = 8
PEER_TOPK = 16

LANES = 128
SUBLANES = 8
CHUNK = 256
NA_QROWS = CHUNK // GRID_W
ROW_TILE = 256
MM_TM = 512
MM_TN = 1024
ADA_TN = 512
PEER_TM = 512
PEER_TE = 512
VMEM_LIMIT = 60 * 1024 * 1024


def _params(sem):
    return pltpu.CompilerParams(dimension_semantics=sem, vmem_limit_bytes=VMEM_LIMIT)


def _mod_row_map(tiles_per_batch, n_batch):
    return lambda i, *_: (jnp.minimum(i // tiles_per_batch, n_batch), 0, 0)


def _ada_kernel(c_ref, w_ref, b_ref, o_ref):
    c = c_ref[...]
    s = (c * jax.nn.sigmoid(c)).astype(BF16)
    o_ref[...] = jnp.dot(s, w_ref[...].astype(BF16), preferred_element_type=F32) + b_ref[...]


def _ada(cvec, ada_w, ada_b):
    depth, d, n6 = ada_w.shape
    rows = cvec.shape[0]
    return pl.pallas_call(
        _ada_kernel,
        grid=(depth, n6 // ADA_TN),
        in_specs=[
            pl.BlockSpec((rows, d), lambda l, j: (0, 0)),
            pl.BlockSpec((None, d, ADA_TN), lambda l, j: (l, 0, j)),
            pl.BlockSpec((None, 1, ADA_TN), lambda l, j: (l, 0, j)),
        ],
        out_specs=pl.BlockSpec((None, rows, ADA_TN), lambda l, j: (l, 0, j)),
        out_shape=jax.ShapeDtypeStruct((depth, rows, n6), F32),
        compiler_params=_params(("arbitrary", "arbitrary")),
        name="ada",
    )(cvec, ada_w, ada_b.reshape(depth, 1, n6))


def _normmod_kernel(x_ref, g_ref, sh_ref, sc_ref, o_ref, *, transpose):
    x = x_ref[...]
    y = x * lax.rsqrt(jnp.mean(x * x, axis=-1, keepdims=True) + NORM_EPS) * g_ref[...]
    h = y * (1.0 + sc_ref[...]) + sh_ref[...]
    o_ref[...] = (h.T if transpose else h).astype(BF16)


def _normmod(x, t_rows, gain, shift, scale, tiles_per_batch, n_batch, transpose):
    d = x.shape[1]
    tm = ROW_TILE
    row_map = _mod_row_map(tiles_per_batch, n_batch)
    if transpose:
        out_spec = pl.BlockSpec((d, tm), lambda i: (0, i))
        out_shape = jax.ShapeDtypeStruct((d, t_rows), BF16)
    else:
        out_spec = pl.BlockSpec((tm, d), lambda i: (i, 0))
        out_shape = jax.ShapeDtypeStruct((t_rows, d), BF16)
    return pl.pallas_call(
        functools.partial(_normmod_kernel, transpose=transpose),
        grid=(t_rows // tm,),
        in_specs=[
            pl.BlockSpec((tm, d), lambda i: (i, 0)),
            pl.BlockSpec((1, d), lambda i: (0, 0)),
            pl.BlockSpec((None, 1, d), row_map),
            pl.BlockSpec((None, 1, d), row_map),
        ],
        out_specs=out_spec,
        out_shape=out_shape,
        compiler_params=_params(("arbitrary",)),
        name="normmod_t" if transpose else "normmod",
    )(x, gain.reshape(1, d), shift, scale)


def _mm_kernel(a_ref, w_ref, o_ref):
    o_ref[...] = jnp.dot(a_ref[...], w_ref[...], preferred_element_type=F32).astype(o_ref.dtype)


def _mm_resid_kernel(a_ref, w_ref, r_ref, g_ref, o_ref):
    acc = jnp.dot(a_ref[...], w_ref[...], preferred_element_type=F32)
    o_ref[...] = r_ref[...] + g_ref[...] * acc


def _matmul(a, w, out_dtype, tn=MM_TN):
    m, k = a.shape
    n = w.shape[1]
    tn = min(tn, n)
    return pl.pallas_call(
        _mm_kernel,
        grid=(m // MM_TM, n // tn),
        in_specs=[
            pl.BlockSpec((MM_TM, k), lambda i, j: (i, 0)),
            pl.BlockSpec((k, tn), lambda i, j: (0, j)),
        ],
        out_specs=pl.BlockSpec((MM_TM, tn), lambda i, j: (i, j)),
        out_shape=jax.ShapeDtypeStruct((m, n), out_dtype),
        compiler_params=_params(("arbitrary", "arbitrary")),
        name="matmul",
    )(a, w)


def _matmul_resid(a, w, resid, gate, tiles_per_batch, n_batch):
    m, k = a.shape
    n = w.shape[1]
    tn = min(MM_TN, n)
    row_map = _mod_row_map(tiles_per_batch, n_batch)
    return pl.pallas_call(
        _mm_resid_kernel,
        grid=(m // MM_TM, n // tn),
        in_specs=[
            pl.BlockSpec((MM_TM, k), lambda i, j: (i, 0)),
            pl.BlockSpec((k, tn), lambda i, j: (0, j)),
            pl.BlockSpec((MM_TM, tn), lambda i, j: (i, j)),
            pl.BlockSpec((None, 1, tn), lambda i, j: row_map(i)[:2] + (j,)),
        ],
        out_specs=pl.BlockSpec((MM_TM, tn), lambda i, j: (i, j)),
        out_shape=jax.ShapeDtypeStruct((m, n), F32),
        compiler_params=_params(("arbitrary", "arbitrary")),
        name="matmul_resid",
    )(a, w, resid, gate)


def _log_sigmoid(x):
    return jnp.minimum(x, 0.0) - jnp.log1p(jnp.exp(-jnp.abs(x)))


def _split_dot(lhs, rhs, split_lhs):
    r = lhs if split_lhs else rhs
    acc = None
    for _ in range(3):
        p = r.astype(BF16)
        t = (jnp.dot(p, rhs, preferred_element_type=F32) if split_lhs
             else jnp.dot(lhs, p, preferred_element_type=F32))
        acc = t if acc is None else acc + t
        r = r - p.astype(F32)
    return acc


def _mlstm_kernel(*refs, reverse, finish, nctx, heads, dk, dv):
    if finish:
        (q_ref, k_ref, v_ref, gc_ref, gr_ref, bc_ref, br_ref, cos_ref, sin_ref,
         hf_ref, og_ref, hg_ref, out_ref, ct_ref, m_ref) = refs
    else:
        (q_ref, k_ref, v_ref, gc_ref, gr_ref, bc_ref, br_ref, cos_ref, sin_ref,
         out_ref, ct_ref, m_ref) = refs
    h = pl.program_id(1)
    c = pl.program_id(2)
    L = CHUNK

    @pl.when(c == 0)
    def _():
        ct_ref[...] = jnp.zeros_like(ct_ref)
        m_ref[...] = jnp.zeros_like(m_ref)

    is_lat = c >= nctx
    cs = jnp.where(is_lat, cos_ref[...], 1.0)
    sn = jnp.where(is_lat, sin_ref[...], 0.0)
    half = dk // 2

    def rope(x):
        x1, x2 = x[:, :half], x[:, half:]
        return jnp.concatenate([x1 * cs - x2 * sn, x2 * cs + x1 * sn], axis=-1)

    qb = (rope(q_ref[...].astype(F32)) * (dk ** -0.5)).astype(BF16)
    kb = rope(k_ref[...].astype(F32)).astype(BF16)

    direction = 1 if reverse else 0
    idx_i = (2 * direction) * heads + h
    idx_f = (2 * direction + 1) * heads + h
    lane = lax.broadcasted_iota(jnp.int32, (1, LANES), 1)
    gc = gc_ref[...] + bc_ref[...]
    i_col = jnp.sum(jnp.where(lane == idx_i, gc, 0.0), axis=1, keepdims=True)
    f_col = jnp.sum(jnp.where(lane == idx_f, gc, 0.0), axis=1, keepdims=True)
    sub = lax.broadcasted_iota(jnp.int32, (4 * heads, 1), 0)
    gr = gr_ref[...] + br_ref[...]
    i_row = jnp.sum(jnp.where(sub == idx_i, gr, 0.0), axis=0, keepdims=True)
    f_row = jnp.sum(jnp.where(sub == idx_f, gr, 0.0), axis=0, keepdims=True)
    lf_col = _log_sigmoid(f_col)
    lf_row = _log_sigmoid(f_row)

    ti = lax.broadcasted_iota(jnp.int32, (L, L), 0)
    si = lax.broadcasted_iota(jnp.int32, (L, L), 1)
    mask = (si >= ti) if reverse else (si <= ti)
    mask_t = (ti >= si) if reverse else (ti <= si)
    tri = jnp.where(mask, 1.0, 0.0).astype(BF16)
    tri_t = jnp.where(mask_t, 1.0, 0.0).astype(BF16)
    b_col = _split_dot(tri, jnp.broadcast_to(lf_col, (L, LANES)), split_lhs=False)[:, 0:1]
    b_row = _split_dot(jnp.broadcast_to(lf_row, (2 * SUBLANES, L)), tri_t, split_lhs=True)[0:1]

    d = jnp.where(mask, b_col - b_row + i_row, -jnp.inf)
    m_prev = m_ref[0:1, 0:1]
    g = b_col + m_prev
    mt = jnp.maximum(g, jnp.max(d, axis=1, keepdims=True))
    s = lax.dot_general(qb, kb, (((1,), (1,)), ((), ())), preferred_element_type=F32)
    s = s * jnp.exp(d - mt)
    inter = jnp.exp(g - mt)

    v_aug = jnp.concatenate([v_ref[...], jnp.ones((L, LANES), BF16)], axis=1)
    nd = (jnp.dot(s.astype(BF16), v_aug, preferred_element_type=F32)
          + inter * jnp.dot(qb, ct_ref[...].astype(BF16), preferred_element_type=F32))
    den = nd[:, dv:dv + 1]
    hout = nd[:, :dv] * (1.0 / jnp.maximum(jnp.abs(den), jnp.exp(-mt)))

    bl = b_col[0:1] if reverse else b_col[L - 1:L]
    wl = bl - b_col + i_col
    m_new = jnp.maximum(bl + m_prev, jnp.max(wl, axis=0, keepdims=True))
    a = jnp.exp(bl + m_prev - m_new)
    w = jnp.exp(wl - m_new)
    vw = (v_aug.astype(F32) * w).astype(BF16)
    ct_ref[...] = a * ct_ref[...] + lax.dot_general(
        kb, vw, (((0,), (0,)), ((), ())), preferred_element_type=F32)
    m_ref[...] = jnp.broadcast_to(m_new, m_ref.shape)

    if finish:
        hs = hf_ref[...] + hout
        y = hs * lax.rsqrt(jnp.mean(hs * hs, axis=-1, keepdims=True) + NORM_EPS) * hg_ref[...]
        out_ref[...] = (y * jax.nn.sigmoid(og_ref[...].astype(F32))).astype(out_ref.dtype)
    else:
        out_ref[...] = hout


def _mlstm_pass(proj, gates_c, gates_r, bias_c, bias_r, cos, sin, n_batch, n_lat, n_ctx,
                reverse, h_fwd=None, head_gain=None):
    t_all = proj.shape[0]
    heads = MLSTM_HEADS
    d = proj.shape[1] // 3
    dk, dv = d // 16, d // 8
    nlat, nctx = n_lat // CHUNK, n_ctx // CHUNK
    finish = h_fwd is not None

    def blk(b, c):
        cc = (nctx - 1 - c) if reverse else c
        lc = (nlat - 1 - (c - nctx)) if reverse else (c - nctx)
        return jnp.where(c < nctx, n_batch * nlat + b * nctx + cc, b * nlat + lc)

    def lat_blk(c):
        lc = (nlat - 1 - (c - nctx)) if reverse else (c - nctx)
        return jnp.clip(lc, 0, nlat - 1)

    in_specs = [
        pl.BlockSpec((CHUNK, dk), lambda b, h, c: (blk(b, c), h)),
        pl.BlockSpec((CHUNK, dk), lambda b, h, c: (blk(b, c), heads + h)),
        pl.BlockSpec((CHUNK, dv), lambda b, h, c: (blk(b, c), heads + h)),
        pl.BlockSpec((CHUNK, LANES), lambda b, h, c: (blk(b, c), 0)),
        pl.BlockSpec((4 * heads, CHUNK), lambda b, h, c: (0, blk(b, c))),
        pl.BlockSpec((1, LANES), lambda b, h, c: (0, 0)),
        pl.BlockSpec((4 * heads, 1), lambda b, h, c: (0, 0)),
        pl.BlockSpec((CHUNK, dk // 2), lambda b, h, c: (lat_blk(c), 0)),
        pl.BlockSpec((CHUNK, dk // 2), lambda b, h, c: (lat_blk(c), 0)),
    ]
    args = [proj, proj, proj, gates_c, gates_r, bias_c, bias_r, cos, sin]
    if finish:
        in_specs += [
            pl.BlockSpec((CHUNK, dv), lambda b, h, c: (blk(b, c), h)),
            pl.BlockSpec((CHUNK, dv), lambda b, h, c: (blk(b, c), 2 * heads + h)),
            pl.BlockSpec((None, 1, dv), lambda b, h, c: (h, 0, 0)),
        ]
        args += [h_fwd, proj, head_gain.reshape(heads, 1, dv)]
    return pl.pallas_call(
        functools.partial(_mlstm_kernel, reverse=reverse, finish=finish, nctx=nctx,
                          heads=heads, dk=dk, dv=dv),
        grid=(n_batch, heads, nctx + nlat),
        in_specs=in_specs,
        out_specs=pl.BlockSpec((CHUNK, dv), lambda b, h, c: (blk(b, c), h)),
        out_shape=jax.ShapeDtypeStruct((t_all, heads * dv), BF16 if finish else F32),
        scratch_shapes=[pltpu.VMEM((dk, dv + LANES), F32), pltpu.VMEM((SUBLANES, LANES), F32)],
        compiler_params=_params(("arbitrary", "arbitrary", "arbitrary")),
        name="mlstm_bwd_finish" if finish else "mlstm_fwd",
    )(*args)


def _na_bias_blocks(rpb, rows):
    qr, span = NA_QROWS, 3 * NA_QROWS
    nh = rpb.shape[0]
    n_dr = 2 * NA_KH - 1
    c = np.arange(GRID_W)[:, None]
    cp = np.arange(GRID_W)[None, :]
    cs = np.clip(c - NA_KW // 2, 0, GRID_W - NA_KW)
    valid_c = (cp >= cs) & (cp < cs + NA_KW)
    onehot = ((cp - c + NA_KW - 1)[..., None] == np.arange(2 * NA_KW - 1)) & valid_c[..., None]
    colmat = jnp.einsum("hrd,cpd->hrcp", rpb, jnp.asarray(onehot, F32), precision=lax.Precision.HIGHEST)
    colmat = jnp.where(valid_c, colmat, -jnp.inf)
    masked = jnp.full((nh, GRID_W, GRID_W), -jnp.inf, F32)
    out = []
    for r0, u0 in ((0, 0), (qr, 0), (rows - qr, rows - span)):
        row_blocks = []
        for rl in range(qr):
            r = r0 + rl
            rs = min(max(r - NA_KH // 2, 0), rows - NA_KH)
            blocks = []
            for i in range(span):
                ia = u0 + i
                dr = ia - r + NA_KH - 1
                blocks.append(colmat[:, dr] if rs <= ia < rs + NA_KH and 0 <= dr < n_dr else masked)
            row_blocks.append(jnp.concatenate(blocks, axis=-1))
        out.append(jnp.concatenate(row_blocks, axis=-2))
    return jnp.stack(out, axis=0)


def _na_kernel(q_ref, k0_ref, k1_ref, k2_ref, v0_ref, v1_ref, v2_ref, kc_ref, vc_ref,
               bias_ref, qg_ref, kg_ref, o_ref):
    hd = q_ref.shape[-1]

    def nrm(ref, gain_ref):
        x = ref[...].astype(F32)
        return x * lax.rsqrt(jnp.mean(x * x, axis=-1, keepdims=True) + NORM_EPS) * gain_ref[...]

    nt = (((1,), (1,)), ((), ()))
    q = (nrm(q_ref, qg_ref) * (hd ** -0.5)).astype(BF16)
    scores = []
    for j, k_ref in enumerate((k0_ref, k1_ref, k2_ref)):
        k = nrm(k_ref, kg_ref).astype(BF16)
        scores.append(lax.dot_general(q, k, nt, preferred_element_type=F32)
                      + bias_ref[:, j * CHUNK:(j + 1) * CHUNK])
    scores.append(lax.dot_general(q, nrm(kc_ref, kg_ref).astype(BF16), nt, preferred_element_type=F32))
    m = scores[0].max(axis=-1, keepdims=True)
    for s in scores[1:]:
        m = jnp.maximum(m, s.max(axis=-1, keepdims=True))
    acc = None
    l = None
    for s, v_ref in zip(scores, (v0_ref, v1_ref, v2_ref, vc_ref)):
        p = jnp.exp(s - m)
        ls = p.sum(axis=-1, keepdims=True)
        pv = jnp.dot(p.astype(BF16), v_ref[...], preferred_element_type=F32)
        l = ls if l is None else l + ls
        acc = pv if acc is None else acc + pv
    o_ref[...] = (acc * (1.0 / l)).astype(o_ref.dtype)


def _na_attention(qkv, bias, q_gain, k_gain, n_batch, n_lat, n_ctx):
    d = qkv.shape[1] // 3
    hd = NA_HEAD_DIM
    nh = d // hd
    nrb = n_lat // CHUNK
    assert n_ctx == CHUNK and nrb >= 3

    def kv_map(off, j):
        return lambda h, b, r: (b * nrb + jnp.clip(r - 1, 0, nrb - 3) + j, off * nh + h)

    def variant(r):
        return jnp.where(r == 0, 0, jnp.where(r == nrb - 1, 2, 1))

    blk = (CHUNK, hd)
    in_specs = [pl.BlockSpec(blk, lambda h, b, r: (b * nrb + r, h))]
    in_specs += [pl.BlockSpec(blk, kv_map(1, j)) for j in range(3)]
    in_specs += [pl.BlockSpec(blk, kv_map(2, j)) for j in range(3)]
    in_specs += [
        pl.BlockSpec(blk, lambda h, b, r: (n_batch * nrb + b, nh + h)),
        pl.BlockSpec(blk, lambda h, b, r: (n_batch * nrb + b, 2 * nh + h)),
        pl.BlockSpec((None, None, CHUNK, 3 * CHUNK), lambda h, b, r: (variant(r), h, 0, 0)),
        pl.BlockSpec((1, hd), lambda h, b, r: (0, 0)),
        pl.BlockSpec((1, hd), lambda h, b, r: (0, 0)),
    ]
    return pl.pallas_call(
        _na_kernel,
        grid=(nh, n_batch, nrb),
        in_specs=in_specs,
        out_specs=pl.BlockSpec(blk, lambda h, b, r: (b * nrb + r, h)),
        out_shape=jax.ShapeDtypeStruct((n_batch * n_lat, d), BF16),
        compiler_params=_params(("arbitrary", "arbitrary", "arbitrary")),
        name="na_attention",
    )(*([qkv] * 9), bias, q_gain.reshape(1, hd), k_gain.reshape(1, hd))


def _sort16_pairs():
    pairs = []

    def merge(lo, n, r):
        step = r * 2
        if step < n:
            merge(lo, n, step)
            merge(lo + r, n, step)
            pairs.extend((i, i + r) for i in range(lo + r, lo + n - r, step))
        else:
            pairs.append((lo, lo + r))

    def sort(lo, n):
        if n > 1:
            sort(lo, n // 2)
            sort(lo + n // 2, n // 2)
            merge(lo, n, 1)

    sort(0, PEER_TOPK)
    return pairs


_SORT16 = _sort16_pairs()


def _bitonic_merge16(x):
    x = list(x)
    d = PEER_TOPK // 2
    while d:
        for i in range(PEER_TOPK):
            if i & d == 0:
                x[i], x[i + d] = jnp.maximum(x[i], x[i + d]), jnp.minimum(x[i], x[i + d])
        d //= 2
    return x


def _merge_top16(top, other):
    n = len(other)
    merged = list(top[:PEER_TOPK - n])
    merged += [jnp.maximum(top[i], other[PEER_TOPK - 1 - i]) for i in range(PEER_TOPK - n, PEER_TOPK)]
    return _bitonic_merge16(merged)


def _top16_sorted(x):
    rows = [x[SUBLANES * r:SUBLANES * (r + 1)] for r in range(PEER_TOPK)]
    for i, j in _SORT16:
        rows[i], rows[j] = jnp.maximum(rows[i], rows[j]), jnp.minimum(rows[i], rows[j])
    shift = SUBLANES // 2
    while shift:
        rolled = [pltpu.roll(v, shift, 0) for v in rows]
        rows = _merge_top16(rows, rolled)
        shift //= 2
    return rows


def _top16_pair_sums(a, b):
    top = [a[0] + b[j] for j in range(PEER_TOPK)]
    for i in range(1, SUBLANES):
        top = _merge_top16(top, [a[i] + b[j] for j in range(PEER_TOPK // (i + 1))])
    return _merge_top16(top, [a[i] + b[0] for i in range(SUBLANES, PEER_TOPK)])


def _peer_select_kernel(ht_ref, wq_ref, k1_ref, k2_ref, s1_ref, s2_ref, st_ref):
    qt = jnp.dot(wq_ref[...], ht_ref[...], preferred_element_type=F32)
    half = qt.shape[0] // 2
    s1 = jnp.dot(k1_ref[...], qt[:half].astype(BF16), preferred_element_type=F32)
    s2 = jnp.dot(k2_ref[...], qt[half:].astype(BF16), preferred_element_type=F32)
    nlb = s1.shape[1] // LANES
    r8 = lax.broadcasted_iota(jnp.int32, (SUBLANES, LANES), 0) % nlb
    tops1, tops2 = [], []
    for lb in range(nlb):
        sl = slice(lb * LANES, (lb + 1) * LANES)
        tops1.append(_top16_sorted(s1[:, sl]))
        tops2.append(_top16_sorted(s2[:, sl]))

    def pack(tops, r):
        out = tops[nlb - 1][r]
        for lb in range(nlb - 2, -1, -1):
            out = jnp.where(r8 == lb, tops[lb][r], out)
        return out

    a = [pack(tops1, r) for r in range(PEER_TOPK)]
    b = [pack(tops2, r) for r in range(PEER_TOPK)]
    a = [v - a[0] for v in a]
    b = [v - b[0] for v in b]
    top = _top16_pair_sums(a, b)
    z = jnp.exp(top[0])
    for v in top[1:]:
        z = z + jnp.exp(v)
    log_z = jnp.log(z)
    b = [v - log_z for v in b]
    st_ref[...] = _top16_pair_sums(a, b)[PEER_TOPK - 1]
    for lb in range(nlb):
        sl = slice(lb * LANES, (lb + 1) * LANES)
        s1_ref[:, sl] = s1[:, sl] - tops1[lb][0][0:1]
        s2_ref[:, sl] = (s2[:, sl] - tops2[lb][0][0:1]) - log_z[lb:lb + 1]


def _peer_select(ht, wq_t, keys1, keys2):
    d, t = ht.shape
    heads = PEER_HEADS
    dk = wq_t.shape[0] // heads
    nk = keys1.shape[0]
    tm = PEER_TM
    assert nk == PEER_TOPK * SUBLANES and tm // LANES <= SUBLANES
    out_s = jax.ShapeDtypeStruct((heads, nk, t), F32)
    return pl.pallas_call(
        _peer_select_kernel,
        grid=(t // tm, heads),
        in_specs=[
            pl.BlockSpec((d, tm), lambda i, h: (0, i)),
            pl.BlockSpec((dk, d), lambda i, h: (h, 0)),
            pl.BlockSpec((nk, dk // 2), lambda i, h: (0, 0)),
            pl.BlockSpec((nk, dk // 2), lambda i, h: (0, 0)),
        ],
        out_specs=[
            pl.BlockSpec((None, nk, tm), lambda i, h: (h, 0, i)),
            pl.BlockSpec((None, nk, tm), lambda i, h: (h, 0, i)),
            pl.BlockSpec((None, None, SUBLANES, LANES), lambda i, h: (h, i, 0, 0)),
        ],
        out_shape=[out_s, out_s, jax.ShapeDtypeStruct((heads, t // tm, SUBLANES, LANES), F32)],
        compiler_params=_params(("arbitrary", "arbitrary")),
        name="peer_select",
    )(ht, wq_t, keys1, keys2)


def _gelu(x):
    return 0.5 * x * (1.0 + lax.erf(x * np.float32(1.0 / np.sqrt(2.0))))


def _peer_dense_kernel(ht_ref, u_ref, vt_ref, s1_ref, s2_ref, st_ref, o_ref, hid0_ref, hid1_ref, act_ref):
    j = pl.program_id(1)
    heads, nk, tm = s2_ref.shape
    te = u_ref.shape[0]
    groups = te // nk
    assert 2 * groups == SUBLANES
    half = nk // 2

    @pl.when(j == 0)
    def _():
        o_ref[...] = jnp.zeros_like(o_ref)
        hid1_ref[...] = jnp.zeros_like(hid1_ref)

    def step(hid_w_ref, hid_r_ref, k1_base):
        hid_w_ref[...] = jnp.dot(u_ref[...], ht_ref[...], preferred_element_type=F32)
        k1_rows = pl.multiple_of((jnp.maximum(j - 1, 0) // 2) * SUBLANES, SUBLANES)
        for lb in range(tm // LANES):
            sl = slice(lb * LANES, (lb + 1) * LANES)
            s1_rows = [s1_ref[h, pl.ds(k1_rows, SUBLANES), sl] for h in range(heads)]
            tau = [st_ref[h, lb:lb + 1, :] for h in range(heads)]
            for kk in range(groups):
                for r0 in (0, half):
                    gate = jnp.zeros((half, LANES), F32)
                    for h in range(heads):
                        s = s1_rows[h][k1_base + kk:k1_base + kk + 1] + s2_ref[h, r0:r0 + half, sl]
                        gate = gate + jnp.where(s >= tau[h], jnp.exp(s), 0.0)
                    rows = slice(kk * nk + r0, kk * nk + r0 + half)
                    act_ref[rows, sl] = (gate * _gelu(hid_r_ref[rows, sl])).astype(BF16)
        o_ref[...] += jnp.dot(vt_ref[...], act_ref[...], preferred_element_type=F32)

    @pl.when(j % 2 == 0)
    def _():
        step(hid0_ref, hid1_ref, groups)

    @pl.when(j % 2 == 1)
    def _():
        step(hid1_ref, hid0_ref, 0)


def _peer_dense(ht, u, vt, s1, s2, stats):
    d, t = ht.shape
    n_exp = u.shape[0]
    heads, nk, _ = s1.shape
    tm, te = PEER_TM, PEER_TE
    nblk = n_exp // te
    return pl.pallas_call(
        _peer_dense_kernel,
        grid=(t // tm, nblk + 1),
        in_specs=[
            pl.BlockSpec((d, tm), lambda i, j: (0, i)),
            pl.BlockSpec((te, d), lambda i, j: (jnp.minimum(j, nblk - 1), 0)),
            pl.BlockSpec((d, te), lambda i, j: (0, jnp.maximum(j - 1, 0))),
            pl.BlockSpec((heads, nk, tm), lambda i, j: (0, 0, i)),
            pl.BlockSpec((heads, nk, tm), lambda i, j: (0, 0, i)),
            pl.BlockSpec((heads, None, SUBLANES, LANES), lambda i, j: (0, i, 0, 0)),
        ],
        out_specs=pl.BlockSpec((d, tm), lambda i, j: (0, i)),
        out_shape=jax.ShapeDtypeStruct((d, t), F32),
        scratch_shapes=[pltpu.VMEM((te, tm), F32), pltpu.VMEM((te, tm), F32), pltpu.VMEM((te, tm), BF16)],
        compiler_params=_params(("arbitrary", "arbitrary")),
        name="peer_dense",
    )(ht, u, vt, s1, s2, stats)


def _resid_t_kernel(x_ref, yt_ref, g_ref, o_ref):
    o_ref[...] = x_ref[...] + g_ref[...] * yt_ref[...].T


def _resid_t(x, yt, gate, tiles_per_batch, n_batch):
    d, t = yt.shape
    tm = ROW_TILE
    return pl.pallas_call(
        _resid_t_kernel,
        grid=(t // tm,),
        in_specs=[
            pl.BlockSpec((tm, d), lambda i: (i, 0)),
            pl.BlockSpec((d, tm), lambda i: (0, i)),
            pl.BlockSpec((None, 1, d), _mod_row_map(tiles_per_batch, n_batch)),
        ],
        out_specs=pl.BlockSpec((tm, d), lambda i: (i, 0)),
        out_shape=jax.ShapeDtypeStruct((t, d), F32),
        compiler_params=_params(("arbitrary",)),
        name="resid_t",
    )(x, yt, gate)


def _peer_ffn(x, t_rows, gain, mods, w_query, sub_keys, u, v, n_lat, n_batch):
    shift, scale, gate = mods
    ht = _normmod(x, t_rows, gain, shift, scale, n_lat // ROW_TILE, n_batch, transpose=True)
    s1, s2, stats = _peer_select(ht, w_query.T.astype(BF16), sub_keys[0].astype(BF16),
                                 sub_keys[1].astype(BF16))
    yt = _peer_dense(ht, u.astype(BF16), v.T.astype(BF16), s1, s2, stats)
    return _resid_t(x, yt, gate, n_lat // ROW_TILE, n_batch)


def _axial_rope(n, head_dim):
    quarter = head_dim // 4
    inv = ROPE_THETA ** (-jnp.arange(quarter, dtype=F32) / quarter)
    t = jnp.arange(n, dtype=jnp.int32)
    row = (t // GRID_W).astype(F32)
    col = (t % GRID_W).astype(F32)
    ang = jnp.concatenate([row[:, None] * inv, col[:, None] * inv], axis=-1)
    return jnp.cos(ang), jnp.sin(ang)


def kernel(x, c, ctx, c_ctx, ada_w, ada_b, norm_mix, norm_ffn, mlstm_w_in, mlstm_gate_b, mlstm_head_gain, mlstm_w_out, na_w_qkv, na_q_gain, na_k_gain, na_rpb, na_w_out, peer_w_query, peer_sub_keys, peer_u, peer_v):
    n_batch, n_lat, d = x.shape
    n_ctx = ctx.shape[1]
    depth = ada_w.shape[0]
    t_lat = n_batch * n_lat
    t_all = t_lat + n_batch * n_ctx
    heads = MLSTM_HEADS
    assert depth == 2 and n_lat % MM_TM == 0 and t_all % MM_TM == 0 and n_ctx % CHUNK == 0

    mod_rows = 2 * SUBLANES
    cvec = jnp.concatenate(
        [c, c_ctx[None, :], jnp.zeros((mod_rows - n_batch - 1, d), F32)], axis=0)
    mod = _ada(cvec, ada_w, ada_b).reshape(depth, mod_rows, N_ADA, d)

    def mods(layer, k):
        return mod[layer, :, k, :].reshape(mod_rows, 1, d)

    xs = jnp.concatenate([x.reshape(t_lat, d), ctx.reshape(n_batch * n_ctx, d)], axis=0)

    hmix = _normmod(xs, t_all, norm_mix[0], mods(0, 0), mods(0, 1), n_lat // ROW_TILE, n_batch,
                    transpose=False)
    w_in = mlstm_w_in[0]
    proj = _matmul(hmix, w_in[:, :3 * d].astype(BF16), BF16)
    w_gate = jnp.pad(w_in[:, 3 * d:], ((0, 0), (0, LANES - 4 * heads))).astype(BF16)
    gates_c = _matmul(hmix, w_gate, F32)
    gates_r = gates_c[:, :4 * heads].T
    bias_c = jnp.pad(mlstm_gate_b[0], (0, LANES - 4 * heads)).reshape(1, LANES)
    bias_r = mlstm_gate_b[0].reshape(4 * heads, 1)
    cos, sin = _axial_rope(n_lat, d // 16)
    scan_args = (proj, gates_c, gates_r, bias_c, bias_r, cos, sin, n_batch, n_lat, n_ctx)
    h_fwd = _mlstm_pass(*scan_args, reverse=False)
    ymix = _mlstm_pass(*scan_args, reverse=True, h_fwd=h_fwd, head_gain=mlstm_head_gain[0])
    xs = _matmul_resid(ymix, mlstm_w_out[0].astype(BF16), xs, mods(0, 2), n_lat // MM_TM, n_batch)
    xs = _peer_ffn(xs, t_all, norm_ffn[0], (mods(0, 3), mods(0, 4), mods(0, 5)),
                   peer_w_query[0], peer_sub_keys[0], peer_u[0], peer_v[0], n_lat, n_batch)

    hmix = _normmod(xs, t_all, norm_mix[1], mods(1, 0), mods(1, 1), n_lat // ROW_TILE, n_batch,
                    transpose=False)
    qkv = _matmul(hmix, na_w_qkv[0].astype(BF16), BF16)
    bias = _na_bias_blocks(na_rpb[0], n_lat // GRID_W)
    att = _na_attention(qkv, bias, na_q_gain[0], na_k_gain[0], n_batch, n_lat, n_ctx)
    xs = _matmul_resid(att, na_w_out[0].astype(BF16), xs, mods(1, 2), n_lat // MM_TM, n_batch)
    xs = _peer_ffn(xs, t_lat, norm_ffn[1], (mods(1, 3), mods(1, 4), mods(1, 5)),
                   peer_w_query[1], peer_sub_keys[1], peer_u[1], peer_v[1], n_lat, n_batch)
    return xs.reshape(n_batch, n_lat, d)
```

```python
import functools

import numpy as np
import jax
import jax.numpy as jnp
from jax import lax
from jax.experimental import pallas as pl
from jax.experimental.pallas import tpu as pltpu

F32 = jnp.float32
BF16 = jnp.bfloat16

NORM_EPS = 1e-6
GRID_W = 64
N_ADA = 6
MLSTM_HEADS = 8
ROPE_THETA = 10000.0
NA_HEAD_DIM = 128
NA_KH = 8
NA_KW = 16
PEER_HEADS = 8
PEER_TOPK = 16

LANES = 128
SUBLANES = 8
CHUNK = 256
NA_QROWS = CHUNK // GRID_W
NA_HEAD_BLOCK = 4
ROW_TILE = 256
MM_TM = 512
MM_TN = 1024
ADA_TN = 512
PEER_TM = 512
PEER_TE = 512
PEER_SUB = 32
VMEM_LIMIT = 60 * 1024 * 1024


def _params(sem):
    return pltpu.CompilerParams(dimension_semantics=sem, vmem_limit_bytes=VMEM_LIMIT)


def _mod_row_map(tiles_per_batch, n_batch):
    return lambda i, *_: (jnp.minimum(i // tiles_per_batch, n_batch), 0, 0)


def _ada_kernel(c_ref, w_ref, b_ref, o_ref):
    c = c_ref[...]
    s = (c * jax.nn.sigmoid(c)).astype(BF16)
    o_ref[...] = jnp.dot(s, w_ref[...].astype(BF16), preferred_element_type=F32) + b_ref[...]


def _ada(cvec, ada_w, ada_b):
    depth, d, n6 = ada_w.shape
    rows = cvec.shape[0]
    return pl.pallas_call(
        _ada_kernel,
        grid=(depth, n6 // ADA_TN),
        in_specs=[
            pl.BlockSpec((rows, d), lambda l, j: (0, 0)),
            pl.BlockSpec((None, d, ADA_TN), lambda l, j: (l, 0, j)),
            pl.BlockSpec((None, 1, ADA_TN), lambda l, j: (l, 0, j)),
        ],
        out_specs=pl.BlockSpec((None, rows, ADA_TN), lambda l, j: (l, 0, j)),
        out_shape=jax.ShapeDtypeStruct((depth, rows, n6), F32),
        compiler_params=_params(("arbitrary", "arbitrary")),
        name="ada",
    )(cvec, ada_w, ada_b.reshape(depth, 1, n6))


def _row_sources(x, tm, width, col_of):
    if not isinstance(x, tuple):
        return [pl.BlockSpec((tm, width), lambda i, *r: (i, col_of(i, *r)))], [x], None
    xa, xb = x
    na = xa.shape[0] // tm
    specs = [
        pl.BlockSpec((tm, width), lambda i, *r: (jnp.minimum(i, na - 1), col_of(i, *r))),
        pl.BlockSpec((tm, width), lambda i, *r: (jnp.maximum(i - na, 0), col_of(i, *r))),
    ]
    return specs, [xa, xb], na


def _read_rows(refs, split):
    if split is None:
        return refs[0][...]
    return jnp.where(pl.program_id(0) < split, refs[0][...], refs[1][...])


def _normmod_kernel(*refs, transpose, split):
    *x_refs, g_ref, sh_ref, sc_ref, o_ref = refs
    x = _read_rows(x_refs, split)
    y = x * lax.rsqrt(jnp.mean(x * x, axis=-1, keepdims=True) + NORM_EPS) * g_ref[...]
    h = y * (1.0 + sc_ref[...]) + sh_ref[...]
    o_ref[...] = (h.T if transpose else h).astype(BF16)


def _normmod(x, t_rows, gain, shift, scale, tiles_per_batch, n_batch, transpose):
    d = gain.shape[0]
    tm = ROW_TILE
    row_map = _mod_row_map(tiles_per_batch, n_batch)
    x_specs, x_args, split = _row_sources(x, tm, d, lambda i: 0)
    if transpose:
        out_spec = pl.BlockSpec((d, tm), lambda i: (0, i))
        out_shape = jax.ShapeDtypeStruct((d, t_rows), BF16)
    else:
        out_spec = pl.BlockSpec((tm, d), lambda i: (i, 0))
        out_shape = jax.ShapeDtypeStruct((t_rows, d), BF16)
    return pl.pallas_call(
        functools.partial(_normmod_kernel, transpose=transpose, split=split),
        grid=(t_rows // tm,),
        in_specs=x_specs + [
            pl.BlockSpec((1, d), lambda i: (0, 0)),
            pl.BlockSpec((None, 1, d), row_map),
            pl.BlockSpec((None, 1, d), row_map),
        ],
        out_specs=out_spec,
        out_shape=out_shape,
        compiler_params=_params(("arbitrary",)),
        name="normmod_t" if transpose else "normmod",
    )(*x_args, gain.reshape(1, d), shift, scale)


def _mm_kernel(a_ref, w_ref, o_ref):
    o_ref[...] = jnp.dot(a_ref[...], w_ref[...], preferred_element_type=F32).astype(o_ref.dtype)


def _mm_headnorm_kernel(a_ref, w_ref, g_ref, o_ref, *, normed_tiles, hd):
    acc = jnp.dot(a_ref[...], w_ref[...], preferred_element_type=F32)
    normed = pl.program_id(1) < normed_tiles
    for c in range(0, acc.shape[1], hd):
        x = acc[:, c:c + hd]
        inv = lax.rsqrt(jnp.mean(x * x, axis=-1, keepdims=True) + NORM_EPS)
        o_ref[:, c:c + hd] = (x * jnp.where(normed, inv, 1.0) * g_ref[...]).astype(o_ref.dtype)


def _mm_resid_kernel(*refs, split):
    a_ref, w_ref, *r_refs, g_ref, o_ref = refs
    acc = jnp.dot(a_ref[...], w_ref[...], preferred_element_type=F32)
    o_ref[...] = _read_rows(r_refs, split) + g_ref[...] * acc


def _matmul(a, w, out_dtype, tn=MM_TN):
    m, k = a.shape
    n = w.shape[1]
    tn = min(tn, n)
    return pl.pallas_call(
        _mm_kernel,
        grid=(m // MM_TM, n // tn),
        in_specs=[
            pl.BlockSpec((MM_TM, k), lambda i, j: (i, 0)),
            pl.BlockSpec((k, tn), lambda i, j: (0, j)),
        ],
        out_specs=pl.BlockSpec((MM_TM, tn), lambda i, j: (i, j)),
        out_shape=jax.ShapeDtypeStruct((m, n), out_dtype),
        compiler_params=_params(("arbitrary", "arbitrary")),
        name="matmul",
    )(a, w)


def _matmul_qkv(a, w, q_gain, k_gain, hd):
    m, k = a.shape
    n = w.shape[1]
    tn = MM_TN
    tiles = n // 3 // tn
    gains = jnp.stack([q_gain * hd ** -0.5, k_gain, jnp.ones_like(k_gain)]).reshape(3, 1, hd)
    return pl.pallas_call(
        functools.partial(_mm_headnorm_kernel, normed_tiles=2 * tiles, hd=hd),
        grid=(m // MM_TM, n // tn),
        in_specs=[
            pl.BlockSpec((MM_TM, k), lambda i, j: (i, 0)),
            pl.BlockSpec((k, tn), lambda i, j: (0, j)),
            pl.BlockSpec((None, 1, hd), lambda i, j: (j // tiles, 0, 0)),
        ],
        out_specs=pl.BlockSpec((MM_TM, tn), lambda i, j: (i, j)),
        out_shape=jax.ShapeDtypeStruct((m, n), BF16),
        compiler_params=_params(("arbitrary", "arbitrary")),
        name="matmul_qkv",
    )(a, w, gains)


def _matmul_resid(a, w, resid, gate, tiles_per_batch, n_batch):
    m, k = a.shape
    n = w.shape[1]
    tn = min(MM_TN, n)
    row_map = _mod_row_map(tiles_per_batch, n_batch)
    r_specs, r_args, split = _row_sources(resid, MM_TM, tn, lambda i, j: j)
    return pl.pallas_call(
        functools.partial(_mm_resid_kernel, split=split),
        grid=(m // MM_TM, n // tn),
        in_specs=[
            pl.BlockSpec((MM_TM, k), lambda i, j: (i, 0)),
            pl.BlockSpec((k, tn), lambda i, j: (0, j)),
        ] + r_specs + [
            pl.BlockSpec((None, 1, tn), lambda i, j: row_map(i)[:2] + (j,)),
        ],
        out_specs=pl.BlockSpec((MM_TM, tn), lambda i, j: (i, j)),
        out_shape=jax.ShapeDtypeStruct((m, n), F32),
        compiler_params=_params(("arbitrary", "arbitrary")),
        name="matmul_resid",
    )(a, w, *r_args, gate)


def _log_sigmoid(x):
    return jnp.minimum(x, 0.0) - jnp.log1p(jnp.exp(-jnp.abs(x)))


def _split_dot(lhs, rhs, split_lhs):
    r = lhs if split_lhs else rhs
    acc = None
    for _ in range(3):
        p = r.astype(BF16)
        t = (jnp.dot(p, rhs, preferred_element_type=F32) if split_lhs
             else jnp.dot(lhs, p, preferred_element_type=F32))
        acc = t if acc is None else acc + t
        r = r - p.astype(F32)
    return acc


def _mlstm_kernel(*refs, reverse, finish, nctx, heads, dk, dv):
    if finish:
        (q_ref, k_ref, v_ref, gc_ref, gr_ref, bc_ref, br_ref, cos_ref, sin_ref,
         hf_ref, og_ref, hg_ref, out_ref, ct_ref, m_ref) = refs
    else:
        (q_ref, k_ref, v_ref, gc_ref, gr_ref, bc_ref, br_ref, cos_ref, sin_ref,
         out_ref, ct_ref, m_ref) = refs
    h = pl.program_id(1)
    c = pl.program_id(2)
    L = CHUNK

    @pl.when(c == 0)
    def _():
        ct_ref[...] = jnp.zeros_like(ct_ref)
        m_ref[...] = jnp.zeros_like(m_ref)

    is_lat = c >= nctx
    cs = jnp.where(is_lat, cos_ref[...], 1.0)
    sn = jnp.where(is_lat, sin_ref[...], 0.0)
    half = dk // 2

    def rope(x):
        x1, x2 = x[:, :half], x[:, half:]
        return jnp.concatenate([x1 * cs - x2 * sn, x2 * cs + x1 * sn], axis=-1)

    qb = (rope(q_ref[...].astype(F32)) * (dk ** -0.5)).astype(BF16)
    kb = rope(k_ref[...].astype(F32)).astype(BF16)

    direction = 1 if reverse else 0
    idx_i = (2 * direction) * heads + h
    idx_f = (2 * direction + 1) * heads + h
    lane = lax.broadcasted_iota(jnp.int32, (1, LANES), 1)
    gc = gc_ref[...] + bc_ref[...]
    i_col = jnp.sum(jnp.where(lane == idx_i, gc, 0.0), axis=1, keepdims=True)
    f_col = jnp.sum(jnp.where(lane == idx_f, gc, 0.0), axis=1, keepdims=True)
    sub = lax.broadcasted_iota(jnp.int32, (4 * heads, 1), 0)
    gr = gr_ref[...] + br_ref[...]
    i_row = jnp.sum(jnp.where(sub == idx_i, gr, 0.0), axis=0, keepdims=True)
    f_row = jnp.sum(jnp.where(sub == idx_f, gr, 0.0), axis=0, keepdims=True)
    lf_col = _log_sigmoid(f_col)
    lf_row = _log_sigmoid(f_row)

    ti = lax.broadcasted_iota(jnp.int32, (L, L), 0)
    si = lax.broadcasted_iota(jnp.int32, (L, L), 1)
    mask = (si >= ti) if reverse else (si <= ti)
    mask_t = (ti >= si) if reverse else (ti <= si)
    tri = jnp.where(mask, 1.0, 0.0).astype(BF16)
    tri_t = jnp.where(mask_t, 1.0, 0.0).astype(BF16)
    b_col = _split_dot(tri, jnp.broadcast_to(lf_col, (L, LANES)), split_lhs=False)[:, 0:1]
    b_row = _split_dot(jnp.broadcast_to(lf_row, (2 * SUBLANES, L)), tri_t, split_lhs=True)[0:1]

    d = jnp.where(mask, b_col - b_row + i_row, -jnp.inf)
    m_prev = m_ref[0:1, 0:1]
    g = b_col + m_prev
    mt = jnp.maximum(g, jnp.max(d, axis=1, keepdims=True))
    s = lax.dot_general(qb, kb, (((1,), (1,)), ((), ())), preferred_element_type=F32)
    s = s * jnp.exp(d - mt)
    inter = jnp.exp(g - mt)

    v_aug = jnp.concatenate([v_ref[...], jnp.ones((L, LANES), BF16)], axis=1)
    nd = (jnp.dot(s.astype(BF16), v_aug, preferred_element_type=F32)
          + inter * jnp.dot(qb, ct_ref[...].astype(BF16), preferred_element_type=F32))
    den = nd[:, dv:dv + 1]
    hout = nd[:, :dv] * (1.0 / jnp.maximum(jnp.abs(den), jnp.exp(-mt)))

    bl = b_col[0:1] if reverse else b_col[L - 1:L]
    wl = bl - b_col + i_col
    m_new = jnp.maximum(bl + m_prev, jnp.max(wl, axis=0, keepdims=True))
    a = jnp.exp(bl + m_prev - m_new)
    w = jnp.exp(wl - m_new)
    vw = (v_aug.astype(F32) * w).astype(BF16)
    ct_ref[...] = a * ct_ref[...] + lax.dot_general(
        kb, vw, (((0,), (0,)), ((), ())), preferred_element_type=F32)
    m_ref[...] = jnp.broadcast_to(m_new, m_ref.shape)

    if finish:
        hs = hf_ref[...] + hout
        y = hs * lax.rsqrt(jnp.mean(hs * hs, axis=-1, keepdims=True) + NORM_EPS) * hg_ref[...]
        out_ref[...] = (y * jax.nn.sigmoid(og_ref[...].astype(F32))).astype(out_ref.dtype)
    else:
        out_ref[...] = hout


def _mlstm_pass(proj, gates_c, gates_r, bias_c, bias_r, cos, sin, n_batch, n_lat, n_ctx,
                reverse, h_fwd=None, head_gain=None):
    t_all = proj.shape[0]
    heads = MLSTM_HEADS
    d = proj.shape[1] // 3
    dk, dv = d // 16, d // 8
    nlat, nctx = n_lat // CHUNK, n_ctx // CHUNK
    finish = h_fwd is not None

    def blk(b, c):
        cc = (nctx - 1 - c) if reverse else c
        lc = (nlat - 1 - (c - nctx)) if reverse else (c - nctx)
        return jnp.where(c < nctx, n_batch * nlat + b * nctx + cc, b * nlat + lc)

    def lat_blk(c):
        lc = (nlat - 1 - (c - nctx)) if reverse else (c - nctx)
        return jnp.clip(lc, 0, nlat - 1)

    in_specs = [
        pl.BlockSpec((CHUNK, dk), lambda b, h, c: (blk(b, c), h)),
        pl.BlockSpec((CHUNK, dk), lambda b, h, c: (blk(b, c), heads + h)),
        pl.BlockSpec((CHUNK, dv), lambda b, h, c: (blk(b, c), heads + h)),
        pl.BlockSpec((CHUNK, LANES), lambda b, h, c: (blk(b, c), 0)),
        pl.BlockSpec((4 * heads, CHUNK), lambda b, h, c: (0, blk(b, c))),
        pl.BlockSpec((1, LANES), lambda b, h, c: (0, 0)),
        pl.BlockSpec((4 * heads, 1), lambda b, h, c: (0, 0)),
        pl.BlockSpec((CHUNK, dk // 2), lambda b, h, c: (lat_blk(c), 0)),
        pl.BlockSpec((CHUNK, dk // 2), lambda b, h, c: (lat_blk(c), 0)),
    ]
    args = [proj, proj, proj, gates_c, gates_r, bias_c, bias_r, cos, sin]
    if finish:
        in_specs += [
            pl.BlockSpec((CHUNK, dv), lambda b, h, c: (blk(b, c), h)),
            pl.BlockSpec((CHUNK, dv), lambda b, h, c: (blk(b, c), 2 * heads + h)),
            pl.BlockSpec((None, 1, dv), lambda b, h, c: (h, 0, 0)),
        ]
        args += [h_fwd, proj, head_gain.reshape(heads, 1, dv)]
    return pl.pallas_call(
        functools.partial(_mlstm_kernel, reverse=reverse, finish=finish, nctx=nctx,
                          heads=heads, dk=dk, dv=dv),
        grid=(n_batch, heads, nctx + nlat),
        in_specs=in_specs,
        out_specs=pl.BlockSpec((CHUNK, dv), lambda b, h, c: (blk(b, c), h)),
        out_shape=jax.ShapeDtypeStruct((t_all, heads * dv), BF16 if finish else F32),
        scratch_shapes=[pltpu.VMEM((dk, dv + LANES), F32), pltpu.VMEM((SUBLANES, LANES), F32)],
        compiler_params=_params(("arbitrary", "arbitrary", "arbitrary")),
        name="mlstm_bwd_finish" if finish else "mlstm_fwd",
    )(*args)


def _na_bias_blocks(rpb, rows):
    qr, span = NA_QROWS, 3 * NA_QROWS
    nh = rpb.shape[0]
    n_dr = 2 * NA_KH - 1
    c = np.arange(GRID_W)[:, None]
    cp = np.arange(GRID_W)[None, :]
    cs = np.clip(c - NA_KW // 2, 0, GRID_W - NA_KW)
    valid_c = (cp >= cs) & (cp < cs + NA_KW)
    onehot = ((cp - c + NA_KW - 1)[..., None] == np.arange(2 * NA_KW - 1)) & valid_c[..., None]
    colmat = jnp.einsum("hrd,cpd->hrcp", rpb, jnp.asarray(onehot, F32), precision=lax.Precision.HIGHEST)
    colmat = jnp.where(valid_c, colmat, -jnp.inf)
    masked = jnp.full((nh, GRID_W, GRID_W), -jnp.inf, F32)
    out = []
    for r0, u0 in ((0, 0), (qr, 0), (rows - qr, rows - span)):
        row_blocks = []
        for rl in range(qr):
            r = r0 + rl
            rs = min(max(r - NA_KH // 2, 0), rows - NA_KH)
            blocks = []
            for i in range(span):
                ia = u0 + i
                dr = ia - r + NA_KH - 1
                blocks.append(colmat[:, dr] if rs <= ia < rs + NA_KH and 0 <= dr < n_dr else masked)
            row_blocks.append(jnp.concatenate(blocks, axis=-1))
        out.append(jnp.concatenate(row_blocks, axis=-2))
    return jnp.stack(out, axis=0)


def _na_kernel(q_ref, k0_ref, k1_ref, k2_ref, v0_ref, v1_ref, v2_ref, kc_ref, vc_ref, bias_ref, o_ref):
    hd = NA_HEAD_DIM
    nt = (((1,), (1,)), ((), ()))
    ones = jnp.ones((CHUNK, hd), BF16)
    for hh in range(NA_HEAD_BLOCK):
        lanes = slice(hh * hd, (hh + 1) * hd)
        q = q_ref[:, lanes]
        scores = [lax.dot_general(q, k_ref[:, lanes], nt, preferred_element_type=F32)
                  + bias_ref[hh, :, j * CHUNK:(j + 1) * CHUNK]
                  for j, k_ref in enumerate((k0_ref, k1_ref, k2_ref))]
        scores.append(lax.dot_general(q, kc_ref[:, lanes], nt, preferred_element_type=F32))
        m = jnp.maximum(jnp.maximum(scores[0], scores[1]), jnp.maximum(scores[2], scores[3]))
        m = m.max(axis=-1, keepdims=True)
        acc = None
        for s, v_ref in zip(scores, (v0_ref, v1_ref, v2_ref, vc_ref)):
            v_aug = jnp.concatenate([v_ref[:, lanes], ones], axis=1)
            pv = jnp.dot(jnp.exp(s - m).astype(BF16), v_aug, preferred_element_type=F32)
            acc = pv if acc is None else acc + pv
        o_ref[:, lanes] = (acc[:, :hd] * (1.0 / acc[:, hd:hd + 1])).astype(o_ref.dtype)


def _na_attention(qkv, bias, n_batch, n_lat, n_ctx):
    d = qkv.shape[1] // 3
    hb = NA_HEAD_BLOCK
    width = hb * NA_HEAD_DIM
    nhb = d // width
    nrb = n_lat // CHUNK
    assert n_ctx == CHUNK and nrb >= 3

    def kv_map(off, j):
        return lambda h, b, r: (b * nrb + jnp.clip(r - 1, 0, nrb - 3) + j, off * nhb + h)

    def variant(r):
        return jnp.where(r == 0, 0, jnp.where(r == nrb - 1, 2, 1))

    blk = (CHUNK, width)
    in_specs = [pl.BlockSpec(blk, lambda h, b, r: (b * nrb + r, h))]
    in_specs += [pl.BlockSpec(blk, kv_map(1, j)) for j in range(3)]
    in_specs += [pl.BlockSpec(blk, kv_map(2, j)) for j in range(3)]
    in_specs += [
        pl.BlockSpec(blk, lambda h, b, r: (n_batch * nrb + b, nhb + h)),
        pl.BlockSpec(blk, lambda h, b, r: (n_batch * nrb + b, 2 * nhb + h)),
        pl.BlockSpec((None, hb, CHUNK, 3 * CHUNK), lambda h, b, r: (variant(r), h, 0, 0)),
    ]
    return pl.pallas_call(
        _na_kernel,
        grid=(nhb, n_batch, nrb),
        in_specs=in_specs,
        out_specs=pl.BlockSpec(blk, lambda h, b, r: (b * nrb + r, h)),
        out_shape=jax.ShapeDtypeStruct((n_batch * n_lat, d), BF16),
        compiler_params=_params(("arbitrary", "arbitrary", "arbitrary")),
        name="na_attention",
    )(*([qkv] * 9), bias)


def _sort16_pairs():
    pairs = []

    def merge(lo, n, r):
        step = r * 2
        if step < n:
            merge(lo, n, step)
            merge(lo + r, n, step)
            pairs.extend((i, i + r) for i in range(lo + r, lo + n - r, step))
        else:
            pairs.append((lo, lo + r))

    def sort(lo, n):
        if n > 1:
            sort(lo, n // 2)
            sort(lo + n // 2, n // 2)
            merge(lo, n, 1)

    sort(0, PEER_TOPK)
    return pairs


_SORT16 = _sort16_pairs()


def _bitonic_merge16(x):
    x = list(x)
    d = PEER_TOPK // 2
    while d:
        for i in range(PEER_TOPK):
            if i & d == 0:
                x[i], x[i + d] = jnp.maximum(x[i], x[i + d]), jnp.minimum(x[i], x[i + d])
        d //= 2
    return x


def _merge_top16(top, other):
    n = len(other)
    merged = list(top[:PEER_TOPK - n])
    merged += [jnp.maximum(top[i], other[PEER_TOPK - 1 - i]) for i in range(PEER_TOPK - n, PEER_TOPK)]
    return _bitonic_merge16(merged)


def _top16_sorted(x):
    rows = [x[SUBLANES * r:SUBLANES * (r + 1)] for r in range(PEER_TOPK)]
    for i, j in _SORT16:
        rows[i], rows[j] = jnp.maximum(rows[i], rows[j]), jnp.minimum(rows[i], rows[j])
    shift = SUBLANES // 2
    while shift:
        rolled = [pltpu.roll(v, shift, 0) for v in rows]
        rows = _merge_top16(rows, rolled)
        shift //= 2
    return rows


def _top16_pair_sums(a, b):
    top = [a[0] + b[j] for j in range(PEER_TOPK)]
    for i in range(1, SUBLANES):
        top = _merge_top16(top, [a[i] + b[j] for j in range(PEER_TOPK // (i + 1))])
    return _merge_top16(top, [a[i] + b[0] for i in range(SUBLANES, PEER_TOPK)])


def _peer_select_kernel(ht_ref, wq_ref, k1_ref, k2_ref, s1_ref, s2_ref, st_ref):
    qt = jnp.dot(wq_ref[...], ht_ref[...], preferred_element_type=F32)
    half = qt.shape[0] // 2
    s1 = jnp.dot(k1_ref[...], qt[:half].astype(BF16), preferred_element_type=F32)
    s2 = jnp.dot(k2_ref[...], qt[half:].astype(BF16), preferred_element_type=F32)
    nlb = s1.shape[1] // LANES
    r8 = lax.broadcasted_iota(jnp.int32, (SUBLANES, LANES), 0) % nlb
    tops1, tops2 = [], []
    for lb in range(nlb):
        sl = slice(lb * LANES, (lb + 1) * LANES)
        tops1.append(_top16_sorted(s1[:, sl]))
        tops2.append(_top16_sorted(s2[:, sl]))

    def pack(tops, r):
        out = tops[nlb - 1][r]
        for lb in range(nlb - 2, -1, -1):
            out = jnp.where(r8 == lb, tops[lb][r], out)
        return out

    a = [pack(tops1, r) for r in range(PEER_TOPK)]
    b = [pack(tops2, r) for r in range(PEER_TOPK)]
    a = [v - a[0] for v in a]
    b = [v - b[0] for v in b]
    top = _top16_pair_sums(a, b)
    z = jnp.exp(top[0])
    for v in top[1:]:
        z = z + jnp.exp(v)
    log_z = jnp.log(z)
    b = [v - log_z for v in b]
    st_ref[...] = _top16_pair_sums(a, b)[PEER_TOPK - 1]
    for lb in range(nlb):
        sl = slice(lb * LANES, (lb + 1) * LANES)
        s1_ref[:, sl] = s1[:, sl] - tops1[lb][0][0:1]
        s2_ref[:, sl] = (s2[:, sl] - tops2[lb][0][0:1]) - log_z[lb:lb + 1]


def _peer_select(ht, wq_t, keys1, keys2):
    d, t = ht.shape
    heads = PEER_HEADS
    dk = wq_t.shape[0] // heads
    nk = keys1.shape[0]
    tm = PEER_TM
    assert nk == PEER_TOPK * SUBLANES and tm // LANES <= SUBLANES
    out_s = jax.ShapeDtypeStruct((heads, nk, t), F32)
    return pl.pallas_call(
        _peer_select_kernel,
        grid=(t // tm, heads),
        in_specs=[
            pl.BlockSpec((d, tm), lambda i, h: (0, i)),
            pl.BlockSpec((dk, d), lambda i, h: (h, 0)),
            pl.BlockSpec((nk, dk // 2), lambda i, h: (0, 0)),
            pl.BlockSpec((nk, dk // 2), lambda i, h: (0, 0)),
        ],
        out_specs=[
            pl.BlockSpec((None, nk, tm), lambda i, h: (h, 0, i)),
            pl.BlockSpec((None, nk, tm), lambda i, h: (h, 0, i)),
            pl.BlockSpec((None, None, SUBLANES, LANES), lambda i, h: (h, i, 0, 0)),
        ],
        out_shape=[out_s, out_s, jax.ShapeDtypeStruct((heads, t // tm, SUBLANES, LANES), F32)],
        compiler_params=_params(("arbitrary", "arbitrary")),
        name="peer_select",
    )(ht, wq_t, keys1, keys2)


def _gelu(x):
    return 0.5 * x * (1.0 + lax.erf(x * np.float32(1.0 / np.sqrt(2.0))))


def _peer_dense_kernel(ht_ref, u_ref, vt_ref, s1_ref, s2_ref, st_ref, o_ref, hid0_ref, hid1_ref, act_ref):
    j = pl.program_id(1)
    heads, nk, tm = s2_ref.shape
    te = u_ref.shape[0]
    groups = te // nk
    assert 2 * groups == s1_ref.shape[1]

    @pl.when(j == 0)
    def _():
        o_ref[...] = jnp.zeros_like(o_ref)
        hid1_ref[...] = jnp.zeros_like(hid1_ref)

    def step(hid_w_ref, hid_r_ref, k1_base):
        hid_w_ref[...] = jnp.dot(u_ref[...], ht_ref[...], preferred_element_type=F32)
        for lb in range(tm // LANES):
            sl = slice(lb * LANES, (lb + 1) * LANES)
            for kk in range(groups):
                k1 = k1_base + kk
                for r0 in range(0, nk, PEER_SUB):
                    gate = jnp.zeros((PEER_SUB, LANES), F32)
                    for h in range(heads):
                        s = s1_ref[h, k1:k1 + 1, sl] + s2_ref[h, r0:r0 + PEER_SUB, sl]
                        gate = gate + jnp.where(s >= st_ref[h, lb:lb + 1, :], jnp.exp(s), 0.0)
                    rows = slice(kk * nk + r0, kk * nk + r0 + PEER_SUB)
                    act_ref[rows, sl] = (gate * _gelu(hid_r_ref[rows, sl])).astype(BF16)
        o_ref[...] += jnp.dot(vt_ref[...], act_ref[...], preferred_element_type=F32)

    @pl.when(j % 2 == 0)
    def _():
        step(hid0_ref, hid1_ref, groups)

    @pl.when(j % 2 == 1)
    def _():
        step(hid1_ref, hid0_ref, 0)


def _peer_dense(ht, u, vt, s1, s2, stats):
    d, t = ht.shape
    n_exp = u.shape[0]
    heads, nk, _ = s1.shape
    tm, te = PEER_TM, PEER_TE
    nblk = n_exp // te
    return pl.pallas_call(
        _peer_dense_kernel,
        grid=(t // tm, nblk + 1),
        in_specs=[
            pl.BlockSpec((d, tm), lambda i, j: (0, i)),
            pl.BlockSpec((te, d), lambda i, j: (jnp.minimum(j, nblk - 1), 0)),
            pl.BlockSpec((d, te), lambda i, j: (0, jnp.maximum(j - 1, 0))),
            pl.BlockSpec((heads, SUBLANES, tm), lambda i, j: (0, jnp.maximum(j - 1, 0) // 2, i)),
            pl.BlockSpec((heads, nk, tm), lambda i, j: (0, 0, i)),
            pl.BlockSpec((heads, None, SUBLANES, LANES), lambda i, j: (0, i, 0, 0)),
        ],
        out_specs=pl.BlockSpec((d, tm), lambda i, j: (0, i)),
        out_shape=jax.ShapeDtypeStruct((d, t), F32),
        scratch_shapes=[pltpu.VMEM((te, tm), F32), pltpu.VMEM((te, tm), F32), pltpu.VMEM((te, tm), BF16)],
        compiler_params=_params(("arbitrary", "arbitrary")),
        name="peer_dense",
    )(ht, u, vt, s1, s2, stats)


def _resid_t_kernel(x_ref, yt_ref, g_ref, o_ref):
    o_ref[...] = x_ref[...] + g_ref[...] * yt_ref[...].T


def _resid_t(x, yt, gate, tiles_per_batch, n_batch):
    d, t = yt.shape
    tm = ROW_TILE
    return pl.pallas_call(
        _resid_t_kernel,
        grid=(t // tm,),
        in_specs=[
            pl.BlockSpec((tm, d), lambda i: (i, 0)),
            pl.BlockSpec((d, tm), lambda i: (0, i)),
            pl.BlockSpec((None, 1, d), _mod_row_map(tiles_per_batch, n_batch)),
        ],
        out_specs=pl.BlockSpec((tm, d), lambda i: (i, 0)),
        out_shape=jax.ShapeDtypeStruct((t, d), F32),
        compiler_params=_params(("arbitrary",)),
        name="resid_t",
    )(x, yt, gate)


def _peer_ffn(x, t_rows, gain, mods, w_query, sub_keys, u, v, n_lat, n_batch):
    shift, scale, gate = mods
    ht = _normmod(x, t_rows, gain, shift, scale, n_lat // ROW_TILE, n_batch, transpose=True)
    s1, s2, stats = _peer_select(ht, w_query.T.astype(BF16), sub_keys[0].astype(BF16),
                                 sub_keys[1].astype(BF16))
    yt = _peer_dense(ht, u.astype(BF16), v.T.astype(BF16), s1, s2, stats)
    return _resid_t(x, yt, gate, n_lat // ROW_TILE, n_batch)


def _axial_rope(n, head_dim):
    quarter = head_dim // 4
    inv = ROPE_THETA ** (-jnp.arange(quarter, dtype=F32) / quarter)
    t = jnp.arange(n, dtype=jnp.int32)
    row = (t // GRID_W).astype(F32)
    col = (t % GRID_W).astype(F32)
    ang = jnp.concatenate([row[:, None] * inv, col[:, None] * inv], axis=-1)
    return jnp.cos(ang), jnp.sin(ang)


def kernel(x, c, ctx, c_ctx, ada_w, ada_b, norm_mix, norm_ffn, mlstm_w_in, mlstm_gate_b, mlstm_head_gain, mlstm_w_out, na_w_qkv, na_q_gain, na_k_gain, na_rpb, na_w_out, peer_w_query, peer_sub_keys, peer_u, peer_v):
    n_batch, n_lat, d = x.shape
    n_ctx = ctx.shape[1]
    depth = ada_w.shape[0]
    t_lat = n_batch * n_lat
    t_all = t_lat + n_batch * n_ctx
    heads = MLSTM_HEADS
    assert depth == 2 and n_lat % MM_TM == 0 and t_all % MM_TM == 0 and n_ctx % CHUNK == 0

    mod_rows = 2 * SUBLANES
    cvec = jnp.concatenate(
        [c, c_ctx[None, :], jnp.zeros((mod_rows - n_batch - 1, d), F32)], axis=0)
    mod = _ada(cvec, ada_w, ada_b).reshape(depth, mod_rows, N_ADA, d)

    def mods(layer, k):
        return mod[layer, :, k, :].reshape(mod_rows, 1, d)

    xs = (x.reshape(t_lat, d), ctx.reshape(n_batch * n_ctx, d))

    hmix = _normmod(xs, t_all, norm_mix[0], mods(0, 0), mods(0, 1), n_lat // ROW_TILE, n_batch,
                    transpose=False)
    w_in = mlstm_w_in[0]
    proj = _matmul(hmix, w_in[:, :3 * d].astype(BF16), BF16)
    w_gate = jnp.pad(w_in[:, 3 * d:], ((0, 0), (0, LANES - 4 * heads))).astype(BF16)
    gates_c = _matmul(hmix, w_gate, F32)
    gates_r = gates_c[:, :4 * heads].T
    bias_c = jnp.pad(mlstm_gate_b[0], (0, LANES - 4 * heads)).reshape(1, LANES)
    bias_r = mlstm_gate_b[0].reshape(4 * heads, 1)
    cos, sin = _axial_rope(n_lat, d // 16)
    scan_args = (proj, gates_c, gates_r, bias_c, bias_r, cos, sin, n_batch, n_lat, n_ctx)
    h_fwd = _mlstm_pass(*scan_args, reverse=False)
    ymix = _mlstm_pass(*scan_args, reverse=True, h_fwd=h_fwd, head_gain=mlstm_head_gain[0])
    xs = _matmul_resid(ymix, mlstm_w_out[0].astype(BF16), xs, mods(0, 2), n_lat // MM_TM, n_batch)
    xs = _peer_ffn(xs, t_all, norm_ffn[0], (mods(0, 3), mods(0, 4), mods(0, 5)),
                   peer_w_query[0], peer_sub_keys[0], peer_u[0], peer_v[0], n_lat, n_batch)

    hmix = _normmod(xs, t_all, norm_mix[1], mods(1, 0), mods(1, 1), n_lat // ROW_TILE, n_batch,
                    transpose=False)
    qkv = _matmul_qkv(hmix, na_w_qkv[0].astype(BF16), na_q_gain[0], na_k_gain[0], NA_HEAD_DIM)
    bias = _na_bias_blocks(na_rpb[0], n_lat // GRID_W)
    att = _na_attention(qkv, bias, n_batch, n_lat, n_ctx)
    xs = _matmul_resid(att, na_w_out[0].astype(BF16), xs, mods(1, 2), n_lat // MM_TM, n_batch)
    xs = _peer_ffn(xs, t_lat, norm_ffn[1], (mods(1, 3), mods(1, 4), mods(1, 5)),
                   peer_w_query[1], peer_sub_keys[1], peer_u[1], peer_v[1], n_lat, n_batch)
    return xs.reshape(n_batch, n_lat, d)
```

```python
import functools

import numpy as np
import jax
import jax.numpy as jnp
from jax import lax
from jax.experimental import pallas as pl
from jax.experimental.pallas import tpu as pltpu

F32 = jnp.float32
BF16 = jnp.bfloat16

NORM_EPS = 1e-6
GRID_W = 64
N_ADA = 6
MLSTM_HEADS = 8
ROPE_THETA = 10000.0
NA_HEAD_DIM = 128
NA_KH = 8
NA_KW = 16
PEER_HEADS = 8
PEER_TOPK = 16

LANES = 128
SUBLANES = 8
CHUNK = 256
NA_QROWS = CHUNK // GRID_W
NA_HEAD_BLOCK = 4
ROW_TILE = 256
MM_TM = 512
MM_TN = 1024
ADA_TN = 512
PEER_TM = 512
PEER_TE = 512
PEER_SUB = 32
VMEM_LIMIT = 60 * 1024 * 1024


def _params(sem):
    return pltpu.CompilerParams(dimension_semantics=sem, vmem_limit_bytes=VMEM_LIMIT)


def _mod_row_map(tiles_per_batch, n_batch):
    return lambda i, *_: (jnp.minimum(i // tiles_per_batch, n_batch), 0, 0)


def _ada_kernel(c_ref, w_ref, b_ref, o_ref):
    c = c_ref[...]
    s = (c * jax.nn.sigmoid(c)).astype(BF16)
    o_ref[...] = jnp.dot(s, w_ref[...].astype(BF16), preferred_element_type=F32) + b_ref[...]


def _ada(cvec, ada_w, ada_b):
    depth, d, n6 = ada_w.shape
    rows = cvec.shape[0]
    return pl.pallas_call(
        _ada_kernel,
        grid=(depth, n6 // ADA_TN),
        in_specs=[
            pl.BlockSpec((rows, d), lambda l, j: (0, 0)),
            pl.BlockSpec((None, d, ADA_TN), lambda l, j: (l, 0, j)),
            pl.BlockSpec((None, 1, ADA_TN), lambda l, j: (l, 0, j)),
        ],
        out_specs=pl.BlockSpec((None, rows, ADA_TN), lambda l, j: (l, 0, j)),
        out_shape=jax.ShapeDtypeStruct((depth, rows, n6), F32),
        compiler_params=_params(("arbitrary", "arbitrary")),
        name="ada",
    )(cvec, ada_w, ada_b.reshape(depth, 1, n6))


def _row_sources(x, tm, width, col_of):
    if not isinstance(x, tuple):
        return [pl.BlockSpec((tm, width), lambda i, *r: (i, col_of(i, *r)))], [x], None
    xa, xb = x
    na = xa.shape[0] // tm
    specs = [
        pl.BlockSpec((tm, width), lambda i, *r: (jnp.minimum(i, na - 1), col_of(i, *r))),
        pl.BlockSpec((tm, width), lambda i, *r: (jnp.maximum(i - na, 0), col_of(i, *r))),
    ]
    return specs, [xa, xb], na


def _read_rows(refs, split):
    if split is None:
        return refs[0][...]
    return jnp.where(pl.program_id(0) < split, refs[0][...], refs[1][...])


def _normmod_kernel(*refs, transpose, split):
    *x_refs, g_ref, sh_ref, sc_ref, o_ref = refs
    x = _read_rows(x_refs, split)
    y = x * lax.rsqrt(jnp.mean(x * x, axis=-1, keepdims=True) + NORM_EPS) * g_ref[...]
    h = y * (1.0 + sc_ref[...]) + sh_ref[...]
    o_ref[...] = (h.T if transpose else h).astype(BF16)


def _normmod(x, t_rows, gain, shift, scale, tiles_per_batch, n_batch, transpose):
    d = gain.shape[0]
    tm = ROW_TILE
    row_map = _mod_row_map(tiles_per_batch, n_batch)
    x_specs, x_args, split = _row_sources(x, tm, d, lambda i: 0)
    if transpose:
        out_spec = pl.BlockSpec((d, tm), lambda i: (0, i))
        out_shape = jax.ShapeDtypeStruct((d, t_rows), BF16)
    else:
        out_spec = pl.BlockSpec((tm, d), lambda i: (i, 0))
        out_shape = jax.ShapeDtypeStruct((t_rows, d), BF16)
    return pl.pallas_call(
        functools.partial(_normmod_kernel, transpose=transpose, split=split),
        grid=(t_rows // tm,),
        in_specs=x_specs + [
            pl.BlockSpec((1, d), lambda i: (0, 0)),
            pl.BlockSpec((None, 1, d), row_map),
            pl.BlockSpec((None, 1, d), row_map),
        ],
        out_specs=out_spec,
        out_shape=out_shape,
        compiler_params=_params(("arbitrary",)),
        name="normmod_t" if transpose else "normmod",
    )(*x_args, gain.reshape(1, d), shift, scale)


def _log_sigmoid(x):
    return jnp.minimum(x, 0.0) - jnp.log1p(jnp.exp(-jnp.abs(x)))


def _mm_kernel(a_ref, w_ref, o_ref):
    o_ref[...] = jnp.dot(a_ref[...], w_ref[...], preferred_element_type=F32).astype(o_ref.dtype)


def _mm_headnorm_kernel(a_ref, w_ref, g_ref, o_ref, *, normed_tiles, hd):
    acc = jnp.dot(a_ref[...], w_ref[...], preferred_element_type=F32)
    normed = pl.program_id(1) < normed_tiles
    for c in range(0, acc.shape[1], hd):
        x = acc[:, c:c + hd]
        inv = lax.rsqrt(jnp.mean(x * x, axis=-1, keepdims=True) + NORM_EPS)
        o_ref[:, c:c + hd] = (x * jnp.where(normed, inv, 1.0) * g_ref[...]).astype(o_ref.dtype)


def _mm_gates_kernel(a_ref, w_ref, b_ref, o_ref, *, heads):
    z = jnp.dot(a_ref[...], w_ref[...], preferred_element_type=F32) + b_ref[...]
    lane = lax.broadcasted_iota(jnp.int32, (1, z.shape[1]), 1)
    is_forget = ((lane // heads) % 2 == 1) & (lane < 4 * heads)
    o_ref[...] = jnp.where(is_forget, _log_sigmoid(z), z)


def _mm_resid_kernel(*refs, split):
    a_ref, w_ref, *r_refs, g_ref, o_ref = refs
    acc = jnp.dot(a_ref[...], w_ref[...], preferred_element_type=F32)
    o_ref[...] = _read_rows(r_refs, split) + g_ref[...] * acc


def _matmul(a, w, out_dtype, tn=MM_TN):
    m, k = a.shape
    n = w.shape[1]
    tn = min(tn, n)
    return pl.pallas_call(
        _mm_kernel,
        grid=(m // MM_TM, n // tn),
        in_specs=[
            pl.BlockSpec((MM_TM, k), lambda i, j: (i, 0)),
            pl.BlockSpec((k, tn), lambda i, j: (0, j)),
        ],
        out_specs=pl.BlockSpec((MM_TM, tn), lambda i, j: (i, j)),
        out_shape=jax.ShapeDtypeStruct((m, n), out_dtype),
        compiler_params=_params(("arbitrary", "arbitrary")),
        name="matmul",
    )(a, w)


def _matmul_gates(a, w, bias, heads):
    m, k = a.shape
    n = w.shape[1]
    return pl.pallas_call(
        functools.partial(_mm_gates_kernel, heads=heads),
        grid=(m // MM_TM,),
        in_specs=[
            pl.BlockSpec((MM_TM, k), lambda i: (i, 0)),
            pl.BlockSpec((k, n), lambda i: (0, 0)),
            pl.BlockSpec((1, n), lambda i: (0, 0)),
        ],
        out_specs=pl.BlockSpec((MM_TM, n), lambda i: (i, 0)),
        out_shape=jax.ShapeDtypeStruct((m, n), F32),
        compiler_params=_params(("arbitrary",)),
        name="matmul_gates",
    )(a, w, bias)


def _matmul_qkv(a, w, q_gain, k_gain, hd):
    m, k = a.shape
    n = w.shape[1]
    tn = MM_TN
    tiles = n // 3 // tn
    gains = jnp.stack([q_gain * hd ** -0.5, k_gain, jnp.ones_like(k_gain)]).reshape(3, 1, hd)
    return pl.pallas_call(
        functools.partial(_mm_headnorm_kernel, normed_tiles=2 * tiles, hd=hd),
        grid=(m // MM_TM, n // tn),
        in_specs=[
            pl.BlockSpec((MM_TM, k), lambda i, j: (i, 0)),
            pl.BlockSpec((k, tn), lambda i, j: (0, j)),
            pl.BlockSpec((None, 1, hd), lambda i, j: (j // tiles, 0, 0)),
        ],
        out_specs=pl.BlockSpec((MM_TM, tn), lambda i, j: (i, j)),
        out_shape=jax.ShapeDtypeStruct((m, n), BF16),
        compiler_params=_params(("arbitrary", "arbitrary")),
        name="matmul_qkv",
    )(a, w, gains)


def _matmul_resid(a, w, resid, gate, tiles_per_batch, n_batch):
    m, k = a.shape
    n = w.shape[1]
    tn = min(MM_TN, n)
    row_map = _mod_row_map(tiles_per_batch, n_batch)
    r_specs, r_args, split = _row_sources(resid, MM_TM, tn, lambda i, j: j)
    return pl.pallas_call(
        functools.partial(_mm_resid_kernel, split=split),
        grid=(m // MM_TM, n // tn),
        in_specs=[
            pl.BlockSpec((MM_TM, k), lambda i, j: (i, 0)),
            pl.BlockSpec((k, tn), lambda i, j: (0, j)),
        ] + r_specs + [
            pl.BlockSpec((None, 1, tn), lambda i, j: row_map(i)[:2] + (j,)),
        ],
        out_specs=pl.BlockSpec((MM_TM, tn), lambda i, j: (i, j)),
        out_shape=jax.ShapeDtypeStruct((m, n), F32),
        compiler_params=_params(("arbitrary", "arbitrary")),
        name="matmul_resid",
    )(a, w, *r_args, gate)


def _split_dot(lhs, rhs, split_lhs):
    r = lhs if split_lhs else rhs
    acc = None
    for _ in range(3):
        p = r.astype(BF16)
        t = (jnp.dot(p, rhs, preferred_element_type=F32) if split_lhs
             else jnp.dot(lhs, p, preferred_element_type=F32))
        acc = t if acc is None else acc + t
        r = r - p.astype(F32)
    return acc


def _mlstm_kernel(*refs, reverse, finish, nctx, heads, dk, dv):
    if finish:
        (q_ref, k_ref, v_ref, gc_ref, gr_ref, cos_ref, sin_ref,
         hf_ref, og_ref, hg_ref, out_ref, ct_ref, m_ref) = refs
    else:
        (q_ref, k_ref, v_ref, gc_ref, gr_ref, cos_ref, sin_ref,
         out_ref, ct_ref, m_ref) = refs
    h = pl.program_id(1)
    c = pl.program_id(2)
    L = CHUNK

    @pl.when(c == 0)
    def _():
        ct_ref[...] = jnp.zeros_like(ct_ref)
        m_ref[...] = jnp.zeros_like(m_ref)

    is_lat = c >= nctx
    cs = jnp.where(is_lat, cos_ref[...], 1.0)
    sn = jnp.where(is_lat, sin_ref[...], 0.0)
    half = dk // 2

    def rope(x):
        x1, x2 = x[:, :half], x[:, half:]
        return jnp.concatenate([x1 * cs - x2 * sn, x2 * cs + x1 * sn], axis=-1)

    qb = (rope(q_ref[...].astype(F32)) * (dk ** -0.5)).astype(BF16)
    kb = rope(k_ref[...].astype(F32)).astype(BF16)

    direction = 1 if reverse else 0
    idx_i = (2 * direction) * heads + h
    idx_f = (2 * direction + 1) * heads + h
    lane = lax.broadcasted_iota(jnp.int32, (1, LANES), 1)
    gc = gc_ref[...]
    i_col = jnp.sum(jnp.where(lane == idx_i, gc, 0.0), axis=1, keepdims=True)
    lf_col = jnp.sum(jnp.where(lane == idx_f, gc, 0.0), axis=1, keepdims=True)
    sub = lax.broadcasted_iota(jnp.int32, (4 * heads, 1), 0)
    gr = gr_ref[...]
    i_row = jnp.sum(jnp.where(sub == idx_i, gr, 0.0), axis=0, keepdims=True)
    lf_row = jnp.sum(jnp.where(sub == idx_f, gr, 0.0), axis=0, keepdims=True)

    ti = lax.broadcasted_iota(jnp.int32, (L, L), 0)
    si = lax.broadcasted_iota(jnp.int32, (L, L), 1)
    mask = (si >= ti) if reverse else (si <= ti)
    mask_t = (ti >= si) if reverse else (ti <= si)
    tri = jnp.where(mask, 1.0, 0.0).astype(BF16)
    tri_t = jnp.where(mask_t, 1.0, 0.0).astype(BF16)
    b_col = _split_dot(tri, jnp.broadcast_to(lf_col, (L, LANES)), split_lhs=False)[:, 0:1]
    b_row = _split_dot(jnp.broadcast_to(lf_row, (2 * SUBLANES, L)), tri_t, split_lhs=True)[0:1]

    d = jnp.where(mask, b_col - b_row + i_row, -jnp.inf)
    m_prev = m_ref[0:1, 0:1]
    g = b_col + m_prev
    mt = jnp.maximum(g, jnp.max(d, axis=1, keepdims=True))
    s = lax.dot_general(qb, kb, (((1,), (1,)), ((), ())), preferred_element_type=F32)
    s = s * jnp.exp(d - mt)
    inter = jnp.exp(g - mt)

    v_aug = jnp.concatenate([v_ref[...], jnp.ones((L, LANES), BF16)], axis=1)
    nd = (jnp.dot(s.astype(BF16), v_aug, preferred_element_type=F32)
          + inter * jnp.dot(qb, ct_ref[...].astype(BF16), preferred_element_type=F32))
    den = nd[:, dv:dv + 1]
    hout = nd[:, :dv] * (1.0 / jnp.maximum(jnp.abs(den), jnp.exp(-mt)))

    bl = b_col[0:1] if reverse else b_col[L - 1:L]
    wl = bl - b_col + i_col
    m_new = jnp.maximum(bl + m_prev, jnp.max(wl, axis=0, keepdims=True))
    a = jnp.exp(bl + m_prev - m_new)
    w = jnp.exp(wl - m_new)
    vw = (v_aug.astype(F32) * w).astype(BF16)
    ct_ref[...] = a * ct_ref[...] + lax.dot_general(
        kb, vw, (((0,), (0,)), ((), ())), preferred_element_type=F32)
    m_ref[...] = jnp.broadcast_to(m_new, m_ref.shape)

    if finish:
        hs = hf_ref[...] + hout
        y = hs * lax.rsqrt(jnp.mean(hs * hs, axis=-1, keepdims=True) + NORM_EPS) * hg_ref[...]
        out_ref[...] = (y * jax.nn.sigmoid(og_ref[...].astype(F32))).astype(out_ref.dtype)
    else:
        out_ref[...] = hout


def _mlstm_pass(proj, gates_c, gates_r, cos, sin, n_batch, n_lat, n_ctx,
                reverse, h_fwd=None, head_gain=None):
    t_all = proj.shape[0]
    heads = MLSTM_HEADS
    d = proj.shape[1] // 3
    dk, dv = d // 16, d // 8
    nlat, nctx = n_lat // CHUNK, n_ctx // CHUNK
    finish = h_fwd is not None

    def blk(b, c):
        cc = (nctx - 1 - c) if reverse else c
        lc = (nlat - 1 - (c - nctx)) if reverse else (c - nctx)
        return jnp.where(c < nctx, n_batch * nlat + b * nctx + cc, b * nlat + lc)

    def lat_blk(c):
        lc = (nlat - 1 - (c - nctx)) if reverse else (c - nctx)
        return jnp.clip(lc, 0, nlat - 1)

    in_specs = [
        pl.BlockSpec((CHUNK, dk), lambda b, h, c: (blk(b, c), h)),
        pl.BlockSpec((CHUNK, dk), lambda b, h, c: (blk(b, c), heads + h)),
        pl.BlockSpec((CHUNK, dv), lambda b, h, c: (blk(b, c), heads + h)),
        pl.BlockSpec((CHUNK, LANES), lambda b, h, c: (blk(b, c), 0)),
        pl.BlockSpec((4 * heads, CHUNK), lambda b, h, c: (0, blk(b, c))),
        pl.BlockSpec((CHUNK, dk // 2), lambda b, h, c: (lat_blk(c), 0)),
        pl.BlockSpec((CHUNK, dk // 2), lambda b, h, c: (lat_blk(c), 0)),
    ]
    args = [proj, proj, proj, gates_c, gates_r, cos, sin]
    if finish:
        in_specs += [
            pl.BlockSpec((CHUNK, dv), lambda b, h, c: (blk(b, c), h)),
            pl.BlockSpec((CHUNK, dv), lambda b, h, c: (blk(b, c), 2 * heads + h)),
            pl.BlockSpec((None, 1, dv), lambda b, h, c: (h, 0, 0)),
        ]
        args += [h_fwd, proj, head_gain.reshape(heads, 1, dv)]
    return pl.pallas_call(
        functools.partial(_mlstm_kernel, reverse=reverse, finish=finish, nctx=nctx,
                          heads=heads, dk=dk, dv=dv),
        grid=(n_batch, heads, nctx + nlat),
        in_specs=in_specs,
        out_specs=pl.BlockSpec((CHUNK, dv), lambda b, h, c: (blk(b, c), h)),
        out_shape=jax.ShapeDtypeStruct((t_all, heads * dv), BF16 if finish else F32),
        scratch_shapes=[pltpu.VMEM((dk, dv + LANES), F32), pltpu.VMEM((SUBLANES, LANES), F32)],
        compiler_params=_params(("arbitrary", "arbitrary", "arbitrary")),
        name="mlstm_bwd_finish" if finish else "mlstm_fwd",
    )(*args)


def _na_bias_blocks(rpb, rows):
    qr, span = NA_QROWS, 3 * NA_QROWS
    nh = rpb.shape[0]
    n_dr = 2 * NA_KH - 1
    c = np.arange(GRID_W)[:, None]
    cp = np.arange(GRID_W)[None, :]
    cs = np.clip(c - NA_KW // 2, 0, GRID_W - NA_KW)
    valid_c = (cp >= cs) & (cp < cs + NA_KW)
    onehot = ((cp - c + NA_KW - 1)[..., None] == np.arange(2 * NA_KW - 1)) & valid_c[..., None]
    colmat = jnp.einsum("hrd,cpd->hrcp", rpb, jnp.asarray(onehot, F32), precision=lax.Precision.HIGHEST)
    colmat = jnp.where(valid_c, colmat, -jnp.inf)
    masked = jnp.full((nh, GRID_W, GRID_W), -jnp.inf, F32)
    out = []
    for r0, u0 in ((0, 0), (qr, 0), (rows - qr, rows - span)):
        row_blocks = []
        for rl in range(qr):
            r = r0 + rl
            rs = min(max(r - NA_KH // 2, 0), rows - NA_KH)
            blocks = []
            for i in range(span):
                ia = u0 + i
                dr = ia - r + NA_KH - 1
                blocks.append(colmat[:, dr] if rs <= ia < rs + NA_KH and 0 <= dr < n_dr else masked)
            row_blocks.append(jnp.concatenate(blocks, axis=-1))
        out.append(jnp.concatenate(row_blocks, axis=-2))
    return jnp.stack(out, axis=0)


def _na_kernel(q_ref, k0_ref, k1_ref, k2_ref, v0_ref, v1_ref, v2_ref, kc_ref, vc_ref, bias_ref, o_ref):
    hd = NA_HEAD_DIM
    nt = (((1,), (1,)), ((), ()))
    ones = jnp.ones((CHUNK, hd), BF16)
    for hh in range(NA_HEAD_BLOCK):
        lanes = slice(hh * hd, (hh + 1) * hd)
        q = q_ref[:, lanes]
        scores = [lax.dot_general(q, k_ref[:, lanes], nt, preferred_element_type=F32)
                  + bias_ref[hh, :, j * CHUNK:(j + 1) * CHUNK]
                  for j, k_ref in enumerate((k0_ref, k1_ref, k2_ref))]
        scores.append(lax.dot_general(q, kc_ref[:, lanes], nt, preferred_element_type=F32))
        m = jnp.maximum(jnp.maximum(scores[0], scores[1]), jnp.maximum(scores[2], scores[3]))
        m = m.max(axis=-1, keepdims=True)
        acc = None
        for s, v_ref in zip(scores, (v0_ref, v1_ref, v2_ref, vc_ref)):
            v_aug = jnp.concatenate([v_ref[:, lanes], ones], axis=1)
            pv = jnp.dot(jnp.exp(s - m).astype(BF16), v_aug, preferred_element_type=F32)
            acc = pv if acc is None else acc + pv
        o_ref[:, lanes] = (acc[:, :hd] * (1.0 / acc[:, hd:hd + 1])).astype(o_ref.dtype)


def _na_attention(qkv, bias, n_batch, n_lat, n_ctx):
    d = qkv.shape[1] // 3
    hb = NA_HEAD_BLOCK
    width = hb * NA_HEAD_DIM
    nhb = d // width
    nrb = n_lat // CHUNK
    assert n_ctx == CHUNK and nrb >= 3

    def kv_map(off, j):
        return lambda h, b, r: (b * nrb + jnp.clip(r - 1, 0, nrb - 3) + j, off * nhb + h)

    def variant(r):
        return jnp.where(r == 0, 0, jnp.where(r == nrb - 1, 2, 1))

    blk = (CHUNK, width)
    in_specs = [pl.BlockSpec(blk, lambda h, b, r: (b * nrb + r, h))]
    in_specs += [pl.BlockSpec(blk, kv_map(1, j)) for j in range(3)]
    in_specs += [pl.BlockSpec(blk, kv_map(2, j)) for j in range(3)]
    in_specs += [
        pl.BlockSpec(blk, lambda h, b, r: (n_batch * nrb + b, nhb + h)),
        pl.BlockSpec(blk, lambda h, b, r: (n_batch * nrb + b, 2 * nhb + h)),
        pl.BlockSpec((None, hb, CHUNK, 3 * CHUNK), lambda h, b, r: (variant(r), h, 0, 0)),
    ]
    return pl.pallas_call(
        _na_kernel,
        grid=(nhb, n_batch, nrb),
        in_specs=in_specs,
        out_specs=pl.BlockSpec(blk, lambda h, b, r: (b * nrb + r, h)),
        out_shape=jax.ShapeDtypeStruct((n_batch * n_lat, d), BF16),
        compiler_params=_params(("arbitrary", "arbitrary", "arbitrary")),
        name="na_attention",
    )(*([qkv] * 9), bias)


def _sort16_pairs():
    pairs = []

    def merge(lo, n, r):
        step = r * 2
        if step < n:
            merge(lo, n, step)
            merge(lo + r, n, step)
            pairs.extend((i, i + r) for i in range(lo + r, lo + n - r, step))
        else:
            pairs.append((lo, lo + r))

    def sort(lo, n):
        if n > 1:
            sort(lo, n // 2)
            sort(lo + n // 2, n // 2)
            merge(lo, n, 1)

    sort(0, PEER_TOPK)
    return pairs


_SORT16 = _sort16_pairs()


def _bitonic_merge16(x):
    x = list(x)
    d = PEER_TOPK // 2
    while d:
        for i in range(PEER_TOPK):
            if i & d == 0:
                x[i], x[i + d] = jnp.maximum(x[i], x[i + d]), jnp.minimum(x[i], x[i + d])
        d //= 2
    return x


def _merge_top16(top, other):
    n = len(other)
    merged = list(top[:PEER_TOPK - n])
    merged += [jnp.maximum(top[i], other[PEER_TOPK - 1 - i]) for i in range(PEER_TOPK - n, PEER_TOPK)]
    return _bitonic_merge16(merged)


def _top16_sorted(x):
    rows = [x[SUBLANES * r:SUBLANES * (r + 1)] for r in range(PEER_TOPK)]
    for i, j in _SORT16:
        rows[i], rows[j] = jnp.maximum(rows[i], rows[j]), jnp.minimum(rows[i], rows[j])
    shift = SUBLANES // 2
    while shift:
        rolled = [pltpu.roll(v, shift, 0) for v in rows]
        rows = _merge_top16(rows, rolled)
        shift //= 2
    return rows


def _top16_pair_sums(a, b):
    top = [a[0] + b[j] for j in range(PEER_TOPK)]
    for i in range(1, SUBLANES):
        top = _merge_top16(top, [a[i] + b[j] for j in range(PEER_TOPK // (i + 1))])
    return _merge_top16(top, [a[i] + b[0] for i in range(SUBLANES, PEER_TOPK)])


def _peer_select_kernel(ht_ref, wq_ref, k1_ref, k2_ref, s1_ref, s2_ref, st_ref, qt_ref):
    heads = s1_ref.shape[0]
    tm = ht_ref.shape[1]
    dk = qt_ref.shape[0] // heads
    half = dk // 2
    nlb = tm // LANES
    r8 = lax.broadcasted_iota(jnp.int32, (SUBLANES, LANES), 0) % nlb
    qt_ref[...] = jnp.dot(wq_ref[...], ht_ref[...], preferred_element_type=F32)

    def head(h, carry):
        row = pl.multiple_of(h * dk, dk)
        q1 = qt_ref[pl.ds(row, half), :].astype(BF16)
        q2 = qt_ref[pl.ds(row + half, half), :].astype(BF16)
        s1 = jnp.dot(k1_ref[...], q1, preferred_element_type=F32)
        s2 = jnp.dot(k2_ref[...], q2, preferred_element_type=F32)
        tops1, tops2 = [], []
        for lb in range(nlb):
            sl = slice(lb * LANES, (lb + 1) * LANES)
            tops1.append(_top16_sorted(s1[:, sl]))
            tops2.append(_top16_sorted(s2[:, sl]))

        def pack(tops, r):
            out = tops[nlb - 1][r]
            for lb in range(nlb - 2, -1, -1):
                out = jnp.where(r8 == lb, tops[lb][r], out)
            return out

        a = [pack(tops1, r) for r in range(PEER_TOPK)]
        b = [pack(tops2, r) for r in range(PEER_TOPK)]
        a = [v - a[0] for v in a]
        b = [v - b[0] for v in b]
        top = _top16_pair_sums(a, b)
        z = jnp.exp(top[0])
        for v in top[1:]:
            z = z + jnp.exp(v)
        log_z = jnp.log(z)
        b = [v - log_z for v in b]
        st_ref[h] = _top16_pair_sums(a, b)[PEER_TOPK - 1]
        for lb in range(nlb):
            sl = slice(lb * LANES, (lb + 1) * LANES)
            s1_ref[h, :, sl] = s1[:, sl] - tops1[lb][0][0:1]
            s2_ref[h, :, sl] = (s2[:, sl] - tops2[lb][0][0:1]) - log_z[lb:lb + 1]
        return carry

    lax.fori_loop(0, heads, head, 0)


def _peer_select(ht, wq_t, keys1, keys2):
    d, t = ht.shape
    heads = PEER_HEADS
    dk = wq_t.shape[0] // heads
    nk = keys1.shape[0]
    tm = PEER_TM
    assert nk == PEER_TOPK * SUBLANES and tm // LANES <= SUBLANES
    out_s = jax.ShapeDtypeStruct((heads, nk, t), F32)
    return pl.pallas_call(
        _peer_select_kernel,
        grid=(t // tm,),
        in_specs=[
            pl.BlockSpec((d, tm), lambda i: (0, i)),
            pl.BlockSpec((heads * dk, d), lambda i: (0, 0)),
            pl.BlockSpec((nk, dk // 2), lambda i: (0, 0)),
            pl.BlockSpec((nk, dk // 2), lambda i: (0, 0)),
        ],
        out_specs=[
            pl.BlockSpec((heads, nk, tm), lambda i: (0, 0, i)),
            pl.BlockSpec((heads, nk, tm), lambda i: (0, 0, i)),
            pl.BlockSpec((heads, None, SUBLANES, LANES), lambda i: (0, i, 0, 0)),
        ],
        out_shape=[out_s, out_s, jax.ShapeDtypeStruct((heads, t // tm, SUBLANES, LANES), F32)],
        scratch_shapes=[pltpu.VMEM((heads * dk, tm), F32)],
        compiler_params=_params(("arbitrary",)),
        name="peer_select",
    )(ht, wq_t, keys1, keys2)


def _gelu(x):
    return 0.5 * x * (1.0 + lax.erf(x * np.float32(1.0 / np.sqrt(2.0))))


def _peer_dense_kernel(ht_ref, u_ref, vt_ref, s1_ref, s2_ref, st_ref, o_ref, hid0_ref, hid1_ref, act_ref):
    j = pl.program_id(1)
    heads, nk, tm = s2_ref.shape
    te = u_ref.shape[0]
    groups = te // nk
    assert 2 * groups == s1_ref.shape[1]

    @pl.when(j == 0)
    def _():
        o_ref[...] = jnp.zeros_like(o_ref)
        hid1_ref[...] = jnp.zeros_like(hid1_ref)

    def step(hid_w_ref, hid_r_ref, k1_base):
        hid_w_ref[...] = jnp.dot(u_ref[...], ht_ref[...], preferred_element_type=F32)
        for lb in range(tm // LANES):
            sl = slice(lb * LANES, (lb + 1) * LANES)
            for kk in range(groups):
                k1 = k1_base + kk
                for r0 in range(0, nk, PEER_SUB):
                    gate = jnp.zeros((PEER_SUB, LANES), F32)
                    for h in range(heads):
                        s = s1_ref[h, k1:k1 + 1, sl] + s2_ref[h, r0:r0 + PEER_SUB, sl]
                        gate = gate + jnp.where(s >= st_ref[h, lb:lb + 1, :], jnp.exp(s), 0.0)
                    rows = slice(kk * nk + r0, kk * nk + r0 + PEER_SUB)
                    act_ref[rows, sl] = (gate * _gelu(hid_r_ref[rows, sl])).astype(BF16)
        o_ref[...] += jnp.dot(vt_ref[...], act_ref[...], preferred_element_type=F32)

    @pl.when(j % 2 == 0)
    def _():
        step(hid0_ref, hid1_ref, groups)

    @pl.when(j % 2 == 1)
    def _():
        step(hid1_ref, hid0_ref, 0)


def _peer_dense(ht, u, vt, s1, s2, stats):
    d, t = ht.shape
    n_exp = u.shape[0]
    heads, nk, _ = s1.shape
    tm, te = PEER_TM, PEER_TE
    nblk = n_exp // te
    return pl.pallas_call(
        _peer_dense_kernel,
        grid=(t // tm, nblk + 1),
        in_specs=[
            pl.BlockSpec((d, tm), lambda i, j: (0, i)),
            pl.BlockSpec((te, d), lambda i, j: (jnp.minimum(j, nblk - 1), 0)),
            pl.BlockSpec((d, te), lambda i, j: (0, jnp.maximum(j - 1, 0))),
            pl.BlockSpec((heads, SUBLANES, tm), lambda i, j: (0, jnp.maximum(j - 1, 0) // 2, i)),
            pl.BlockSpec((heads, nk, tm), lambda i, j: (0, 0, i)),
            pl.BlockSpec((heads, None, SUBLANES, LANES), lambda i, j: (0, i, 0, 0)),
        ],
        out_specs=pl.BlockSpec((d, tm), lambda i, j: (0, i)),
        out_shape=jax.ShapeDtypeStruct((d, t), F32),
        scratch_shapes=[pltpu.VMEM((te, tm), F32), pltpu.VMEM((te, tm), F32), pltpu.VMEM((te, tm), BF16)],
        compiler_params=_params(("arbitrary", "arbitrary")),
        name="peer_dense",
    )(ht, u, vt, s1, s2, stats)


def _resid_t_kernel(x_ref, yt_ref, g_ref, o_ref):
    o_ref[...] = x_ref[...] + g_ref[...] * yt_ref[...].T


def _resid_t(x, yt, gate, tiles_per_batch, n_batch):
    d, t = yt.shape
    tm = ROW_TILE
    return pl.pallas_call(
        _resid_t_kernel,
        grid=(t // tm,),
        in_specs=[
            pl.BlockSpec((tm, d), lambda i: (i, 0)),
            pl.BlockSpec((d, tm), lambda i: (0, i)),
            pl.BlockSpec((None, 1, d), _mod_row_map(tiles_per_batch, n_batch)),
        ],
        out_specs=pl.BlockSpec((tm, d), lambda i: (i, 0)),
        out_shape=jax.ShapeDtypeStruct((t, d), F32),
        compiler_params=_params(("arbitrary",)),
        name="resid_t",
    )(x, yt, gate)


def _peer_ffn(x, t_rows, gain, mods, w_query, sub_keys, u, v, n_lat, n_batch):
    shift, scale, gate = mods
    ht = _normmod(x, t_rows, gain, shift, scale, n_lat // ROW_TILE, n_batch, transpose=True)
    s1, s2, stats = _peer_select(ht, w_query.T.astype(BF16), sub_keys[0].astype(BF16),
                                 sub_keys[1].astype(BF16))
    yt = _peer_dense(ht, u.astype(BF16), v.T.astype(BF16), s1, s2, stats)
    return _resid_t(x, yt, gate, n_lat // ROW_TILE, n_batch)


def _axial_rope(n, head_dim):
    quarter = head_dim // 4
    inv = ROPE_THETA ** (-jnp.arange(quarter, dtype=F32) / quarter)
    t = jnp.arange(n, dtype=jnp.int32)
    row = (t // GRID_W).astype(F32)
    col = (t % GRID_W).astype(F32)
    ang = jnp.concatenate([row[:, None] * inv, col[:, None] * inv], axis=-1)
    return jnp.cos(ang), jnp.sin(ang)


def kernel(x, c, ctx, c_ctx, ada_w, ada_b, norm_mix, norm_ffn, mlstm_w_in, mlstm_gate_b, mlstm_head_gain, mlstm_w_out, na_w_qkv, na_q_gain, na_k_gain, na_rpb, na_w_out, peer_w_query, peer_sub_keys, peer_u, peer_v):
    n_batch, n_lat, d = x.shape
    n_ctx = ctx.shape[1]
    depth = ada_w.shape[0]
    t_lat = n_batch * n_lat
    t_all = t_lat + n_batch * n_ctx
    heads = MLSTM_HEADS
    assert depth == 2 and n_lat % MM_TM == 0 and t_all % MM_TM == 0 and n_ctx % CHUNK == 0

    mod_rows = 2 * SUBLANES
    cvec = jnp.concatenate(
        [c, c_ctx[None, :], jnp.zeros((mod_rows - n_batch - 1, d), F32)], axis=0)
    mod = _ada(cvec, ada_w, ada_b).reshape(depth, mod_rows, N_ADA, d)

    def mods(layer, k):
        return mod[layer, :, k, :].reshape(mod_rows, 1, d)

    xs = (x.reshape(t_lat, d), ctx.reshape(n_batch * n_ctx, d))

    hmix = _normmod(xs, t_all, norm_mix[0], mods(0, 0), mods(0, 1), n_lat // ROW_TILE, n_batch,
                    transpose=False)
    w_in = mlstm_w_in[0]
    proj = _matmul(hmix, w_in[:, :3 * d].astype(BF16), BF16)
    w_gate = jnp.pad(w_in[:, 3 * d:], ((0, 0), (0, LANES - 4 * heads))).astype(BF16)
    gate_b = jnp.pad(mlstm_gate_b[0], (0, LANES - 4 * heads)).reshape(1, LANES)
    gates_c = _matmul_gates(hmix, w_gate, gate_b, heads)
    gates_r = gates_c[:, :4 * heads].T
    cos, sin = _axial_rope(n_lat, d // 16)
    scan_args = (proj, gates_c, gates_r, cos, sin, n_batch, n_lat, n_ctx)
    h_fwd = _mlstm_pass(*scan_args, reverse=False)
    ymix = _mlstm_pass(*scan_args, reverse=True, h_fwd=h_fwd, head_gain=mlstm_head_gain[0])
    xs = _matmul_resid(ymix, mlstm_w_out[0].astype(BF16), xs, mods(0, 2), n_lat // MM_TM, n_batch)
    xs = _peer_ffn(xs, t_all, norm_ffn[0], (mods(0, 3), mods(0, 4), mods(0, 5)),
                   peer_w_query[0], peer_sub_keys[0], peer_u[0], peer_v[0], n_lat, n_batch)

    hmix = _normmod(xs, t_all, norm_mix[1], mods(1, 0), mods(1, 1), n_lat // ROW_TILE, n_batch,
                    transpose=False)
    qkv = _matmul_qkv(hmix, na_w_qkv[0].astype(BF16), na_q_gain[0], na_k_gain[0], NA_HEAD_DIM)
    bias = _na_bias_blocks(na_rpb[0], n_lat // GRID_W)
    att = _na_attention(qkv, bias, n_batch, n_lat, n_ctx)
    xs = _matmul_resid(att, na_w_out[0].astype(BF16), xs, mods(1, 2), n_lat // MM_TM, n_batch)
    xs = _peer_ffn(xs, t_lat, norm_ffn[1], (mods(1, 3), mods(1, 4), mods(1, 5)),
                   peer_w_query[1], peer_sub_keys[1], peer_u[1], peer_v[1], n_lat, n_batch)
    return xs.reshape(n_batch, n_lat, d)
```

```python
import functools

import numpy as np
import jax
import jax.numpy as jnp
from jax import lax
from jax.experimental import pallas as pl
from jax.experimental.pallas import tpu as pltpu

F32 = jnp.float32
BF16 = jnp.bfloat16

NORM_EPS = 1e-6
GRID_W = 64
N_ADA = 6
MLSTM_HEADS = 8
ROPE_THETA = 10000.0
NA_HEAD_DIM = 128
NA_KH = 8
NA_KW = 16
PEER_HEADS = 8
PEER_TOPK = 16

LANES = 128
SUBLANES = 8
CHUNK = 256
NA_QROWS = CHUNK // GRID_W
NA_HEAD_BLOCK = 4
ROW_TILE = 256
MM_TM = 512
MM_TN = 2048
ADA_TN = 512
PEER_TM = 512
PEER_TE = 512
PEER_SUB = 32
VMEM_LIMIT = 60 * 1024 * 1024


def _params(sem):
    return pltpu.CompilerParams(dimension_semantics=sem, vmem_limit_bytes=VMEM_LIMIT)


def _mod_row_map(tiles_per_batch, n_batch):
    return lambda i, *_: (jnp.minimum(i // tiles_per_batch, n_batch), 0, 0)


def _ada_kernel(c_ref, w_ref, b_ref, o_ref):
    c = c_ref[...]
    s = (c * jax.nn.sigmoid(c)).astype(BF16)
    o_ref[...] = jnp.dot(s, w_ref[...].astype(BF16), preferred_element_type=F32) + b_ref[...]


def _ada(cvec, ada_w, ada_b):
    depth, d, n6 = ada_w.shape
    rows = cvec.shape[0]
    return pl.pallas_call(
        _ada_kernel,
        grid=(depth, n6 // ADA_TN),
        in_specs=[
            pl.BlockSpec((rows, d), lambda l, j: (0, 0)),
            pl.BlockSpec((None, d, ADA_TN), lambda l, j: (l, 0, j)),
            pl.BlockSpec((None, 1, ADA_TN), lambda l, j: (l, 0, j)),
        ],
        out_specs=pl.BlockSpec((None, rows, ADA_TN), lambda l, j: (l, 0, j)),
        out_shape=jax.ShapeDtypeStruct((depth, rows, n6), F32),
        compiler_params=_params(("arbitrary", "arbitrary")),
        name="ada",
    )(cvec, ada_w, ada_b.reshape(depth, 1, n6))


def _row_sources(x, tm, width, col_of):
    if not isinstance(x, tuple):
        return [pl.BlockSpec((tm, width), lambda i, *r: (i, col_of(i, *r)))], [x], None
    xa, xb = x
    na = xa.shape[0] // tm
    specs = [
        pl.BlockSpec((tm, width), lambda i, *r: (jnp.minimum(i, na - 1), col_of(i, *r))),
        pl.BlockSpec((tm, width), lambda i, *r: (jnp.maximum(i - na, 0), col_of(i, *r))),
    ]
    return specs, [xa, xb], na


def _read_rows(refs, split):
    if split is None:
        return refs[0][...]
    return jnp.where(pl.program_id(0) < split, refs[0][...], refs[1][...])


def _normmod_kernel(*refs, transpose, split):
    *x_refs, g_ref, sh_ref, sc_ref, o_ref = refs
    x = _read_rows(x_refs, split)
    y = x * lax.rsqrt(jnp.mean(x * x, axis=-1, keepdims=True) + NORM_EPS) * g_ref[...]
    h = y * (1.0 + sc_ref[...]) + sh_ref[...]
    o_ref[...] = (h.T if transpose else h).astype(BF16)


def _normmod(x, t_rows, gain, shift, scale, tiles_per_batch, n_batch, transpose):
    d = gain.shape[0]
    tm = ROW_TILE
    row_map = _mod_row_map(tiles_per_batch, n_batch)
    x_specs, x_args, split = _row_sources(x, tm, d, lambda i: 0)
    if transpose:
        out_spec = pl.BlockSpec((d, tm), lambda i: (0, i))
        out_shape = jax.ShapeDtypeStruct((d, t_rows), BF16)
    else:
        out_spec = pl.BlockSpec((tm, d), lambda i: (i, 0))
        out_shape = jax.ShapeDtypeStruct((t_rows, d), BF16)
    return pl.pallas_call(
        functools.partial(_normmod_kernel, transpose=transpose, split=split),
        grid=(t_rows // tm,),
        in_specs=x_specs + [
            pl.BlockSpec((1, d), lambda i: (0, 0)),
            pl.BlockSpec((None, 1, d), row_map),
            pl.BlockSpec((None, 1, d), row_map),
        ],
        out_specs=out_spec,
        out_shape=out_shape,
        compiler_params=_params(("arbitrary",)),
        name="normmod_t" if transpose else "normmod",
    )(*x_args, gain.reshape(1, d), shift, scale)


def _log_sigmoid(x):
    return jnp.minimum(x, 0.0) - jnp.log1p(jnp.exp(-jnp.abs(x)))


def _mm_kernel(a_ref, w_ref, o_ref):
    o_ref[...] = jnp.dot(a_ref[...], w_ref[...], preferred_element_type=F32).astype(o_ref.dtype)


def _mm_headnorm_kernel(a_ref, w_ref, g_ref, o_ref, *, normed_tiles, hd):
    acc = jnp.dot(a_ref[...], w_ref[...], preferred_element_type=F32)
    normed = pl.program_id(1) < normed_tiles
    for c in range(0, acc.shape[1], hd):
        x = acc[:, c:c + hd]
        inv = lax.rsqrt(jnp.mean(x * x, axis=-1, keepdims=True) + NORM_EPS)
        o_ref[:, c:c + hd] = (x * jnp.where(normed, inv, 1.0) * g_ref[...]).astype(o_ref.dtype)


def _mm_gates_kernel(a_ref, w_ref, b_ref, o_ref, *, heads):
    z = jnp.dot(a_ref[...], w_ref[...], preferred_element_type=F32) + b_ref[...]
    lane = lax.broadcasted_iota(jnp.int32, (1, z.shape[1]), 1)
    is_forget = ((lane // heads) % 2 == 1) & (lane < 4 * heads)
    o_ref[...] = jnp.where(is_forget, _log_sigmoid(z), z)


def _mm_resid_kernel(*refs, split):
    a_ref, w_ref, *r_refs, g_ref, o_ref = refs
    acc = jnp.dot(a_ref[...], w_ref[...], preferred_element_type=F32)
    o_ref[...] = _read_rows(r_refs, split) + g_ref[...] * acc


def _matmul(a, w, out_dtype, tn=MM_TN):
    m, k = a.shape
    n = w.shape[1]
    tn = min(tn, n)
    return pl.pallas_call(
        _mm_kernel,
        grid=(m // MM_TM, n // tn),
        in_specs=[
            pl.BlockSpec((MM_TM, k), lambda i, j: (i, 0)),
            pl.BlockSpec((k, tn), lambda i, j: (0, j)),
        ],
        out_specs=pl.BlockSpec((MM_TM, tn), lambda i, j: (i, j)),
        out_shape=jax.ShapeDtypeStruct((m, n), out_dtype),
        compiler_params=_params(("arbitrary", "arbitrary")),
        name="matmul",
    )(a, w)


def _matmul_gates(a, w, bias, heads):
    m, k = a.shape
    n = w.shape[1]
    return pl.pallas_call(
        functools.partial(_mm_gates_kernel, heads=heads),
        grid=(m // MM_TM,),
        in_specs=[
            pl.BlockSpec((MM_TM, k), lambda i: (i, 0)),
            pl.BlockSpec((k, n), lambda i: (0, 0)),
            pl.BlockSpec((1, n), lambda i: (0, 0)),
        ],
        out_specs=pl.BlockSpec((MM_TM, n), lambda i: (i, 0)),
        out_shape=jax.ShapeDtypeStruct((m, n), F32),
        compiler_params=_params(("arbitrary",)),
        name="matmul_gates",
    )(a, w, bias)


def _matmul_qkv(a, w, q_gain, k_gain, hd):
    m, k = a.shape
    n = w.shape[1]
    tn = MM_TN
    tiles = n // 3 // tn
    gains = jnp.stack([q_gain * hd ** -0.5, k_gain, jnp.ones_like(k_gain)]).reshape(3, 1, hd)
    return pl.pallas_call(
        functools.partial(_mm_headnorm_kernel, normed_tiles=2 * tiles, hd=hd),
        grid=(m // MM_TM, n // tn),
        in_specs=[
            pl.BlockSpec((MM_TM, k), lambda i, j: (i, 0)),
            pl.BlockSpec((k, tn), lambda i, j: (0, j)),
            pl.BlockSpec((None, 1, hd), lambda i, j: (j // tiles, 0, 0)),
        ],
        out_specs=pl.BlockSpec((MM_TM, tn), lambda i, j: (i, j)),
        out_shape=jax.ShapeDtypeStruct((m, n), BF16),
        compiler_params=_params(("arbitrary", "arbitrary")),
        name="matmul_qkv",
    )(a, w, gains)


def _matmul_resid(a, w, resid, gate, tiles_per_batch, n_batch):
    m, k = a.shape
    n = w.shape[1]
    tn = min(MM_TN // 2, n)
    row_map = _mod_row_map(tiles_per_batch, n_batch)
    r_specs, r_args, split = _row_sources(resid, MM_TM, tn, lambda i, j: j)
    return pl.pallas_call(
        functools.partial(_mm_resid_kernel, split=split),
        grid=(m // MM_TM, n // tn),
        in_specs=[
            pl.BlockSpec((MM_TM, k), lambda i, j: (i, 0)),
            pl.BlockSpec((k, tn), lambda i, j: (0, j)),
        ] + r_specs + [
            pl.BlockSpec((None, 1, tn), lambda i, j: row_map(i)[:2] + (j,)),
        ],
        out_specs=pl.BlockSpec((MM_TM, tn), lambda i, j: (i, j)),
        out_shape=jax.ShapeDtypeStruct((m, n), F32),
        compiler_params=_params(("arbitrary", "arbitrary")),
        name="matmul_resid",
    )(a, w, *r_args, gate)


def _split_dot(lhs, rhs, split_lhs):
    r = lhs if split_lhs else rhs
    acc = None
    for _ in range(3):
        p = r.astype(BF16)
        t = (jnp.dot(p, rhs, preferred_element_type=F32) if split_lhs
             else jnp.dot(lhs, p, preferred_element_type=F32))
        acc = t if acc is None else acc + t
        r = r - p.astype(F32)
    return acc


def _mlstm_kernel(*refs, reverse, finish, nctx, heads, dk, dv):
    if finish:
        (q_ref, k_ref, v_ref, gc_ref, gr_ref, cos_ref, sin_ref,
         hf_ref, og_ref, hg_ref, out_ref, ct_ref, m_ref) = refs
    else:
        (q_ref, k_ref, v_ref, gc_ref, gr_ref, cos_ref, sin_ref,
         out_ref, ct_ref, m_ref) = refs
    h = pl.program_id(1)
    c = pl.program_id(2)
    L = CHUNK

    @pl.when(c == 0)
    def _():
        ct_ref[...] = jnp.zeros_like(ct_ref)
        m_ref[...] = jnp.zeros_like(m_ref)

    is_lat = c >= nctx
    cs = jnp.where(is_lat, cos_ref[...], 1.0)
    sn = jnp.where(is_lat, sin_ref[...], 0.0)
    half = dk // 2

    def rope(x):
        x1, x2 = x[:, :half], x[:, half:]
        return jnp.concatenate([x1 * cs - x2 * sn, x2 * cs + x1 * sn], axis=-1)

    qb = (rope(q_ref[...].astype(F32)) * (dk ** -0.5)).astype(BF16)
    kb = rope(k_ref[...].astype(F32)).astype(BF16)

    direction = 1 if reverse else 0
    idx_i = (2 * direction) * heads + h
    idx_f = (2 * direction + 1) * heads + h
    lane = lax.broadcasted_iota(jnp.int32, (1, LANES), 1)
    gc = gc_ref[...]
    i_col = jnp.sum(jnp.where(lane == idx_i, gc, 0.0), axis=1, keepdims=True)
    lf_col = jnp.sum(jnp.where(lane == idx_f, gc, 0.0), axis=1, keepdims=True)
    sub = lax.broadcasted_iota(jnp.int32, (4 * heads, 1), 0)
    gr = gr_ref[...]
    i_row = jnp.sum(jnp.where(sub == idx_i, gr, 0.0), axis=0, keepdims=True)
    lf_row = jnp.sum(jnp.where(sub == idx_f, gr, 0.0), axis=0, keepdims=True)

    ti = lax.broadcasted_iota(jnp.int32, (L, L), 0)
    si = lax.broadcasted_iota(jnp.int32, (L, L), 1)
    mask = (si >= ti) if reverse else (si <= ti)
    mask_t = (ti >= si) if reverse else (ti <= si)
    tri = jnp.where(mask, 1.0, 0.0).astype(BF16)
    tri_t = jnp.where(mask_t, 1.0, 0.0).astype(BF16)
    b_col = _split_dot(tri, jnp.broadcast_to(lf_col, (L, LANES)), split_lhs=False)[:, 0:1]
    b_row = _split_dot(jnp.broadcast_to(lf_row, (2 * SUBLANES, L)), tri_t, split_lhs=True)[0:1]

    d = jnp.where(mask, b_col - b_row + i_row, -jnp.inf)
    m_prev = m_ref[0:1, 0:1]
    g = b_col + m_prev
    mt = jnp.maximum(g, jnp.max(d, axis=1, keepdims=True))
    s = lax.dot_general(qb, kb, (((1,), (1,)), ((), ())), preferred_element_type=F32)
    s = s * jnp.exp(d - mt)
    inter = jnp.exp(g - mt)

    v_aug = jnp.concatenate([v_ref[...], jnp.ones((L, LANES), BF16)], axis=1)
    nd = (jnp.dot(s.astype(BF16), v_aug, preferred_element_type=F32)
          + inter * jnp.dot(qb, ct_ref[...].astype(BF16), preferred_element_type=F32))
    den = nd[:, dv:dv + 1]
    hout = nd[:, :dv] * (1.0 / jnp.maximum(jnp.abs(den), jnp.exp(-mt)))

    bl = b_col[0:1] if reverse else b_col[L - 1:L]
    wl = bl - b_col + i_col
    m_new = jnp.maximum(bl + m_prev, jnp.max(wl, axis=0, keepdims=True))
    a = jnp.exp(bl + m_prev - m_new)
    w = jnp.exp(wl - m_new)
    vw = (v_aug.astype(F32) * w).astype(BF16)
    ct_ref[...] = a * ct_ref[...] + lax.dot_general(
        kb, vw, (((0,), (0,)), ((), ())), preferred_element_type=F32)
    m_ref[...] = jnp.broadcast_to(m_new, m_ref.shape)

    if finish:
        hs = hf_ref[...] + hout
        y = hs * lax.rsqrt(jnp.mean(hs * hs, axis=-1, keepdims=True) + NORM_EPS) * hg_ref[...]
        out_ref[...] = (y * jax.nn.sigmoid(og_ref[...].astype(F32))).astype(out_ref.dtype)
    else:
        out_ref[...] = hout


def _mlstm_pass(proj, gates_c, gates_r, cos, sin, n_batch, n_lat, n_ctx,
                reverse, h_fwd=None, head_gain=None):
    t_all = proj.shape[0]
    heads = MLSTM_HEADS
    d = proj.shape[1] // 3
    dk, dv = d // 16, d // 8
    nlat, nctx = n_lat // CHUNK, n_ctx // CHUNK
    finish = h_fwd is not None

    def blk(b, c):
        cc = (nctx - 1 - c) if reverse else c
        lc = (nlat - 1 - (c - nctx)) if reverse else (c - nctx)
        return jnp.where(c < nctx, n_batch * nlat + b * nctx + cc, b * nlat + lc)

    def lat_blk(c):
        lc = (nlat - 1 - (c - nctx)) if reverse else (c - nctx)
        return jnp.clip(lc, 0, nlat - 1)

    in_specs = [
        pl.BlockSpec((CHUNK, dk), lambda b, h, c: (blk(b, c), h)),
        pl.BlockSpec((CHUNK, dk), lambda b, h, c: (blk(b, c), heads + h)),
        pl.BlockSpec((CHUNK, dv), lambda b, h, c: (blk(b, c), heads + h)),
        pl.BlockSpec((CHUNK, LANES), lambda b, h, c: (blk(b, c), 0)),
        pl.BlockSpec((4 * heads, CHUNK), lambda b, h, c: (0, blk(b, c))),
        pl.BlockSpec((CHUNK, dk // 2), lambda b, h, c: (lat_blk(c), 0)),
        pl.BlockSpec((CHUNK, dk // 2), lambda b, h, c: (lat_blk(c), 0)),
    ]
    args = [proj, proj, proj, gates_c, gates_r, cos, sin]
    if finish:
        in_specs += [
            pl.BlockSpec((CHUNK, dv), lambda b, h, c: (blk(b, c), h)),
            pl.BlockSpec((CHUNK, dv), lambda b, h, c: (blk(b, c), 2 * heads + h)),
            pl.BlockSpec((None, 1, dv), lambda b, h, c: (h, 0, 0)),
        ]
        args += [h_fwd, proj, head_gain.reshape(heads, 1, dv)]
    return pl.pallas_call(
        functools.partial(_mlstm_kernel, reverse=reverse, finish=finish, nctx=nctx,
                          heads=heads, dk=dk, dv=dv),
        grid=(n_batch, heads, nctx + nlat),
        in_specs=in_specs,
        out_specs=pl.BlockSpec((CHUNK, dv), lambda b, h, c: (blk(b, c), h)),
        out_shape=jax.ShapeDtypeStruct((t_all, heads * dv), BF16 if finish else F32),
        scratch_shapes=[pltpu.VMEM((dk, dv + LANES), F32), pltpu.VMEM((SUBLANES, LANES), F32)],
        compiler_params=_params(("arbitrary", "arbitrary", "arbitrary")),
        name="mlstm_bwd_finish" if finish else "mlstm_fwd",
    )(*args)


def _na_bias_blocks(rpb, rows):
    qr, span = NA_QROWS, 3 * NA_QROWS
    nh = rpb.shape[0]
    n_dr = 2 * NA_KH - 1
    c = np.arange(GRID_W)[:, None]
    cp = np.arange(GRID_W)[None, :]
    cs = np.clip(c - NA_KW // 2, 0, GRID_W - NA_KW)
    valid_c = (cp >= cs) & (cp < cs + NA_KW)
    onehot = ((cp - c + NA_KW - 1)[..., None] == np.arange(2 * NA_KW - 1)) & valid_c[..., None]
    colmat = jnp.einsum("hrd,cpd->hrcp", rpb, jnp.asarray(onehot, F32), precision=lax.Precision.HIGHEST)
    colmat = jnp.where(valid_c, colmat, -jnp.inf)
    masked = jnp.full((nh, GRID_W, GRID_W), -jnp.inf, F32)
    out = []
    for r0, u0 in ((0, 0), (qr, 0), (rows - qr, rows - span)):
        row_blocks = []
        for rl in range(qr):
            r = r0 + rl
            rs = min(max(r - NA_KH // 2, 0), rows - NA_KH)
            blocks = []
            for i in range(span):
                ia = u0 + i
                dr = ia - r + NA_KH - 1
                blocks.append(colmat[:, dr] if rs <= ia < rs + NA_KH and 0 <= dr < n_dr else masked)
            row_blocks.append(jnp.concatenate(blocks, axis=-1))
        out.append(jnp.concatenate(row_blocks, axis=-2))
    return jnp.stack(out, axis=0)


def _na_kernel(q_ref, k0_ref, k1_ref, k2_ref, v0_ref, v1_ref, v2_ref, kc_ref, vc_ref, bias_ref, o_ref):
    hd = NA_HEAD_DIM
    nt = (((1,), (1,)), ((), ()))
    ones = jnp.ones((CHUNK, hd), BF16)
    for hh in range(NA_HEAD_BLOCK):
        lanes = slice(hh * hd, (hh + 1) * hd)
        q = q_ref[:, lanes]
        scores = [lax.dot_general(q, k_ref[:, lanes], nt, preferred_element_type=F32)
                  + bias_ref[hh, :, j * CHUNK:(j + 1) * CHUNK]
                  for j, k_ref in enumerate((k0_ref, k1_ref, k2_ref))]
        scores.append(lax.dot_general(q, kc_ref[:, lanes], nt, preferred_element_type=F32))
        m = jnp.maximum(jnp.maximum(scores[0], scores[1]), jnp.maximum(scores[2], scores[3]))
        m = m.max(axis=-1, keepdims=True)
        acc = None
        for s, v_ref in zip(scores, (v0_ref, v1_ref, v2_ref, vc_ref)):
            v_aug = jnp.concatenate([v_ref[:, lanes], ones], axis=1)
            pv = jnp.dot(jnp.exp(s - m).astype(BF16), v_aug, preferred_element_type=F32)
            acc = pv if acc is None else acc + pv
        o_ref[:, lanes] = (acc[:, :hd] * (1.0 / acc[:, hd:hd + 1])).astype(o_ref.dtype)


def _na_attention(qkv, bias, n_batch, n_lat, n_ctx):
    d = qkv.shape[1] // 3
    hb = NA_HEAD_BLOCK
    width = hb * NA_HEAD_DIM
    nhb = d // width
    nrb = n_lat // CHUNK
    assert n_ctx == CHUNK and nrb >= 3

    def kv_map(off, j):
        return lambda h, b, r: (b * nrb + jnp.clip(r - 1, 0, nrb - 3) + j, off * nhb + h)

    def variant(r):
        return jnp.where(r == 0, 0, jnp.where(r == nrb - 1, 2, 1))

    blk = (CHUNK, width)
    in_specs = [pl.BlockSpec(blk, lambda h, b, r: (b * nrb + r, h))]
    in_specs += [pl.BlockSpec(blk, kv_map(1, j)) for j in range(3)]
    in_specs += [pl.BlockSpec(blk, kv_map(2, j)) for j in range(3)]
    in_specs += [
        pl.BlockSpec(blk, lambda h, b, r: (n_batch * nrb + b, nhb + h)),
        pl.BlockSpec(blk, lambda h, b, r: (n_batch * nrb + b, 2 * nhb + h)),
        pl.BlockSpec((None, hb, CHUNK, 3 * CHUNK), lambda h, b, r: (variant(r), h, 0, 0)),
    ]
    return pl.pallas_call(
        _na_kernel,
        grid=(nhb, n_batch, nrb),
        in_specs=in_specs,
        out_specs=pl.BlockSpec(blk, lambda h, b, r: (b * nrb + r, h)),
        out_shape=jax.ShapeDtypeStruct((n_batch * n_lat, d), BF16),
        compiler_params=_params(("arbitrary", "arbitrary", "arbitrary")),
        name="na_attention",
    )(*([qkv] * 9), bias)


def _sort16_pairs():
    pairs = []

    def merge(lo, n, r):
        step = r * 2
        if step < n:
            merge(lo, n, step)
            merge(lo + r, n, step)
            pairs.extend((i, i + r) for i in range(lo + r, lo + n - r, step))
        else:
            pairs.append((lo, lo + r))

    def sort(lo, n):
        if n > 1:
            sort(lo, n // 2)
            sort(lo + n // 2, n // 2)
            merge(lo, n, 1)

    sort(0, PEER_TOPK)
    return pairs


_SORT16 = _sort16_pairs()


def _bitonic_merge16(x):
    x = list(x)
    d = PEER_TOPK // 2
    while d:
        for i in range(PEER_TOPK):
            if i & d == 0:
                x[i], x[i + d] = jnp.maximum(x[i], x[i + d]), jnp.minimum(x[i], x[i + d])
        d //= 2
    return x


def _merge_top16(top, other):
    n = len(other)
    merged = list(top[:PEER_TOPK - n])
    merged += [jnp.maximum(top[i], other[PEER_TOPK - 1 - i]) for i in range(PEER_TOPK - n, PEER_TOPK)]
    return _bitonic_merge16(merged)


def _top16_sorted(x):
    rows = [x[SUBLANES * r:SUBLANES * (r + 1)] for r in range(PEER_TOPK)]
    for i, j in _SORT16:
        rows[i], rows[j] = jnp.maximum(rows[i], rows[j]), jnp.minimum(rows[i], rows[j])
    shift = SUBLANES // 2
    while shift:
        rolled = [pltpu.roll(v, shift, 0) for v in rows]
        rows = _merge_top16(rows, rolled)
        shift //= 2
    return rows


def _top16_pair_sums(a, b):
    top = [a[0] + b[j] for j in range(PEER_TOPK)]
    for i in range(1, SUBLANES):
        top = _merge_top16(top, [a[i] + b[j] for j in range(PEER_TOPK // (i + 1))])
    return _merge_top16(top, [a[i] + b[0] for i in range(SUBLANES, PEER_TOPK)])


def _peer_select_kernel(ht_ref, wq_ref, k1_ref, k2_ref, s1_ref, s2_ref, st_ref, qt_ref):
    heads = s1_ref.shape[0]
    tm = ht_ref.shape[1]
    dk = qt_ref.shape[0] // heads
    half = dk // 2
    nlb = tm // LANES
    r8 = lax.broadcasted_iota(jnp.int32, (SUBLANES, LANES), 0) % nlb
    qt_ref[...] = jnp.dot(wq_ref[...], ht_ref[...], preferred_element_type=F32)

    def head(h, carry):
        row = pl.multiple_of(h * dk, dk)
        q1 = qt_ref[pl.ds(row, half), :].astype(BF16)
        q2 = qt_ref[pl.ds(row + half, half), :].astype(BF16)
        s1 = jnp.dot(k1_ref[...], q1, preferred_element_type=F32)
        s2 = jnp.dot(k2_ref[...], q2, preferred_element_type=F32)
        tops1, tops2 = [], []
        for lb in range(nlb):
            sl = slice(lb * LANES, (lb + 1) * LANES)
            tops1.append(_top16_sorted(s1[:, sl]))
            tops2.append(_top16_sorted(s2[:, sl]))

        def pack(tops, r):
            out = tops[nlb - 1][r]
            for lb in range(nlb - 2, -1, -1):
                out = jnp.where(r8 == lb, tops[lb][r], out)
            return out

        a = [pack(tops1, r) for r in range(PEER_TOPK)]
        b = [pack(tops2, r) for r in range(PEER_TOPK)]
        a = [v - a[0] for v in a]
        b = [v - b[0] for v in b]
        top = _top16_pair_sums(a, b)
        z = jnp.exp(top[0])
        for v in top[1:]:
            z = z + jnp.exp(v)
        log_z = jnp.log(z)
        b = [v - log_z for v in b]
        st_ref[h] = _top16_pair_sums(a, b)[PEER_TOPK - 1]
        for lb in range(nlb):
            sl = slice(lb * LANES, (lb + 1) * LANES)
            s1_ref[h, :, sl] = s1[:, sl] - tops1[lb][0][0:1]
            s2_ref[h, :, sl] = (s2[:, sl] - tops2[lb][0][0:1]) - log_z[lb:lb + 1]
        return carry

    lax.fori_loop(0, heads, head, 0)


def _peer_select(ht, wq_t, keys1, keys2):
    d, t = ht.shape
    heads = PEER_HEADS
    dk = wq_t.shape[0] // heads
    nk = keys1.shape[0]
    tm = PEER_TM
    assert nk == PEER_TOPK * SUBLANES and tm // LANES <= SUBLANES
    out_s = jax.ShapeDtypeStruct((heads, nk, t), F32)
    return pl.pallas_call(
        _peer_select_kernel,
        grid=(t // tm,),
        in_specs=[
            pl.BlockSpec((d, tm), lambda i: (0, i)),
            pl.BlockSpec((heads * dk, d), lambda i: (0, 0)),
            pl.BlockSpec((nk, dk // 2), lambda i: (0, 0)),
            pl.BlockSpec((nk, dk // 2), lambda i: (0, 0)),
        ],
        out_specs=[
            pl.BlockSpec((heads, nk, tm), lambda i: (0, 0, i)),
            pl.BlockSpec((heads, nk, tm), lambda i: (0, 0, i)),
            pl.BlockSpec((heads, None, SUBLANES, LANES), lambda i: (0, i, 0, 0)),
        ],
        out_shape=[out_s, out_s, jax.ShapeDtypeStruct((heads, t // tm, SUBLANES, LANES), F32)],
        scratch_shapes=[pltpu.VMEM((heads * dk, tm), F32)],
        compiler_params=_params(("arbitrary",)),
        name="peer_select",
    )(ht, wq_t, keys1, keys2)


def _gelu(x):
    return 0.5 * x * (1.0 + lax.erf(x * np.float32(1.0 / np.sqrt(2.0))))


def _peer_dense_kernel(ht_ref, u_ref, vt_ref, s1_ref, s2_ref, st_ref, o_ref, hid0_ref, hid1_ref, act_ref):
    j = pl.program_id(1)
    heads, nk, tm = s2_ref.shape
    te = u_ref.shape[0]
    groups = te // nk
    assert 2 * groups == s1_ref.shape[1]

    @pl.when(j == 0)
    def _():
        o_ref[...] = jnp.zeros_like(o_ref)
        hid1_ref[...] = jnp.zeros_like(hid1_ref)

    def step(hid_w_ref, hid_r_ref, k1_base):
        hid_w_ref[...] = jnp.dot(u_ref[...], ht_ref[...], preferred_element_type=F32)
        for lb in range(tm // LANES):
            sl = slice(lb * LANES, (lb + 1) * LANES)
            for kk in range(groups):
                k1 = k1_base + kk
                for r0 in range(0, nk, PEER_SUB):
                    gate = jnp.zeros((PEER_SUB, LANES), F32)
                    for h in range(heads):
                        s = s1_ref[h, k1:k1 + 1, sl] + s2_ref[h, r0:r0 + PEER_SUB, sl]
                        gate = gate + jnp.where(s >= st_ref[h, lb:lb + 1, :], jnp.exp(s), 0.0)
                    rows = slice(kk * nk + r0, kk * nk + r0 + PEER_SUB)
                    act_ref[rows, sl] = (gate * _gelu(hid_r_ref[rows, sl])).astype(BF16)
        o_ref[...] += jnp.dot(vt_ref[...], act_ref[...], preferred_element_type=F32)

    @pl.when(j % 2 == 0)
    def _():
        step(hid0_ref, hid1_ref, groups)

    @pl.when(j % 2 == 1)
    def _():
        step(hid1_ref, hid0_ref, 0)


def _peer_dense(ht, u, vt, s1, s2, stats):
    d, t = ht.shape
    n_exp = u.shape[0]
    heads, nk, _ = s1.shape
    tm, te = PEER_TM, PEER_TE
    nblk = n_exp // te
    return pl.pallas_call(
        _peer_dense_kernel,
        grid=(t // tm, nblk + 1),
        in_specs=[
            pl.BlockSpec((d, tm), lambda i, j: (0, i)),
            pl.BlockSpec((te, d), lambda i, j: (jnp.minimum(j, nblk - 1), 0)),
            pl.BlockSpec((d, te), lambda i, j: (0, jnp.maximum(j - 1, 0))),
            pl.BlockSpec((heads, SUBLANES, tm), lambda i, j: (0, jnp.maximum(j - 1, 0) // 2, i)),
            pl.BlockSpec((heads, nk, tm), lambda i, j: (0, 0, i)),
            pl.BlockSpec((heads, None, SUBLANES, LANES), lambda i, j: (0, i, 0, 0)),
        ],
        out_specs=pl.BlockSpec((d, tm), lambda i, j: (0, i)),
        out_shape=jax.ShapeDtypeStruct((d, t), F32),
        scratch_shapes=[pltpu.VMEM((te, tm), F32), pltpu.VMEM((te, tm), F32), pltpu.VMEM((te, tm), BF16)],
        compiler_params=_params(("arbitrary", "arbitrary")),
        name="peer_dense",
    )(ht, u, vt, s1, s2, stats)


def _resid_t_kernel(x_ref, yt_ref, g_ref, o_ref):
    o_ref[...] = x_ref[...] + g_ref[...] * yt_ref[...].T


def _resid_t(x, yt, gate, tiles_per_batch, n_batch):
    d, t = yt.shape
    tm = ROW_TILE
    return pl.pallas_call(
        _resid_t_kernel,
        grid=(t // tm,),
        in_specs=[
            pl.BlockSpec((tm, d), lambda i: (i, 0)),
            pl.BlockSpec((d, tm), lambda i: (0, i)),
            pl.BlockSpec((None, 1, d), _mod_row_map(tiles_per_batch, n_batch)),
        ],
        out_specs=pl.BlockSpec((tm, d), lambda i: (i, 0)),
        out_shape=jax.ShapeDtypeStruct((t, d), F32),
        compiler_params=_params(("arbitrary",)),
        name="resid_t",
    )(x, yt, gate)


def _peer_ffn(x, t_rows, gain, mods, w_query, sub_keys, u, v, n_lat, n_batch):
    shift, scale, gate = mods
    ht = _normmod(x, t_rows, gain, shift, scale, n_lat // ROW_TILE, n_batch, transpose=True)
    s1, s2, stats = _peer_select(ht, w_query.T.astype(BF16), sub_keys[0].astype(BF16),
                                 sub_keys[1].astype(BF16))
    yt = _peer_dense(ht, u.astype(BF16), v.T.astype(BF16), s1, s2, stats)
    return _resid_t(x, yt, gate, n_lat // ROW_TILE, n_batch)


def _axial_rope(n, head_dim):
    quarter = head_dim // 4
    inv = ROPE_THETA ** (-jnp.arange(quarter, dtype=F32) / quarter)
    t = jnp.arange(n, dtype=jnp.int32)
    row = (t // GRID_W).astype(F32)
    col = (t % GRID_W).astype(F32)
    ang = jnp.concatenate([row[:, None] * inv, col[:, None] * inv], axis=-1)
    return jnp.cos(ang), jnp.sin(ang)


def kernel(x, c, ctx, c_ctx, ada_w, ada_b, norm_mix, norm_ffn, mlstm_w_in, mlstm_gate_b, mlstm_head_gain, mlstm_w_out, na_w_qkv, na_q_gain, na_k_gain, na_rpb, na_w_out, peer_w_query, peer_sub_keys, peer_u, peer_v):
    n_batch, n_lat, d = x.shape
    n_ctx = ctx.shape[1]
    depth = ada_w.shape[0]
    t_lat = n_batch * n_lat
    t_all = t_lat + n_batch * n_ctx
    heads = MLSTM_HEADS
    assert depth == 2 and n_lat % MM_TM == 0 and t_all % MM_TM == 0 and n_ctx % CHUNK == 0

    mod_rows = 2 * SUBLANES
    cvec = jnp.concatenate(
        [c, c_ctx[None, :], jnp.zeros((mod_rows - n_batch - 1, d), F32)], axis=0)
    mod = _ada(cvec, ada_w, ada_b).reshape(depth, mod_rows, N_ADA, d)

    def mods(layer, k):
        return mod[layer, :, k, :].reshape(mod_rows, 1, d)

    xs = (x.reshape(t_lat, d), ctx.reshape(n_batch * n_ctx, d))

    hmix = _normmod(xs, t_all, norm_mix[0], mods(0, 0), mods(0, 1), n_lat // ROW_TILE, n_batch,
                    transpose=False)
    w_in = mlstm_w_in[0]
    proj = _matmul(hmix, w_in[:, :3 * d].astype(BF16), BF16)
    w_gate = jnp.pad(w_in[:, 3 * d:], ((0, 0), (0, LANES - 4 * heads))).astype(BF16)
    gate_b = jnp.pad(mlstm_gate_b[0], (0, LANES - 4 * heads)).reshape(1, LANES)
    gates_c = _matmul_gates(hmix, w_gate, gate_b, heads)
    gates_r = gates_c[:, :4 * heads].T
    cos, sin = _axial_rope(n_lat, d // 16)
    scan_args = (proj, gates_c, gates_r, cos, sin, n_batch, n_lat, n_ctx)
    h_fwd = _mlstm_pass(*scan_args, reverse=False)
    ymix = _mlstm_pass(*scan_args, reverse=True, h_fwd=h_fwd, head_gain=mlstm_head_gain[0])
    xs = _matmul_resid(ymix, mlstm_w_out[0].astype(BF16), xs, mods(0, 2), n_lat // MM_TM, n_batch)
    xs = _peer_ffn(xs, t_all, norm_ffn[0], (mods(0, 3), mods(0, 4), mods(0, 5)),
                   peer_w_query[0], peer_sub_keys[0], peer_u[0], peer_v[0], n_lat, n_batch)

    hmix = _normmod(xs, t_all, norm_mix[1], mods(1, 0), mods(1, 1), n_lat // ROW_TILE, n_batch,
                    transpose=False)
    qkv = _matmul_qkv(hmix, na_w_qkv[0].astype(BF16), na_q_gain[0], na_k_gain[0], NA_HEAD_DIM)
    bias = _na_bias_blocks(na_rpb[0], n_lat // GRID_W)
    att = _na_attention(qkv, bias, n_batch, n_lat, n_ctx)
    xs = _matmul_resid(att, na_w_out[0].astype(BF16), xs, mods(1, 2), n_lat // MM_TM, n_batch)
    xs = _peer_ffn(xs, t_lat, norm_ffn[1], (mods(1, 3), mods(1, 4), mods(1, 5)),
                   peer_w_query[1], peer_sub_keys[1], peer_u[1], peer_v[1], n_lat, n_batch)
    return xs.reshape(n_batch, n_lat, d)
```

```python
import functools

import numpy as np
import jax
import jax.numpy as jnp
from jax import lax
from jax.experimental import pallas as pl
from jax.experimental.pallas import tpu as pltpu

F32 = jnp.float32
BF16 = jnp.bfloat16

NORM_EPS = 1e-6
GRID_W = 64
N_ADA = 6
MLSTM_HEADS = 8
ROPE_THETA = 10000.0
NA_HEAD_DIM = 128
NA_KH = 8
NA_KW = 16
PEER_HEADS = 8
PEER_TOPK = 16

LANES = 128
SUBLANES = 8
CHUNK = 256
NA_QROWS = CHUNK // GRID_W
NA_HEAD_BLOCK = 4
ROW_TILE = 256
MM_TM = 512
MM_TN = 2048
ADA_TN = 512
PEER_TM = 512
PEER_TE = 512
PEER_SUB = 32
VMEM_LIMIT = 60 * 1024 * 1024


def _params(sem):
    return pltpu.CompilerParams(dimension_semantics=sem, vmem_limit_bytes=VMEM_LIMIT)


def _mod_row_map(tiles_per_batch, n_batch):
    return lambda i, *_: (jnp.minimum(i // tiles_per_batch, n_batch), 0, 0)


def _ada_kernel(c_ref, w_ref, b_ref, o_ref):
    c = c_ref[...]
    s = (c * jax.nn.sigmoid(c)).astype(BF16)
    o_ref[...] = jnp.dot(s, w_ref[...].astype(BF16), preferred_element_type=F32) + b_ref[...]


def _ada(cvec, ada_w, ada_b):
    depth, d, n6 = ada_w.shape
    rows = cvec.shape[0]
    return pl.pallas_call(
        _ada_kernel,
        grid=(depth, n6 // ADA_TN),
        in_specs=[
            pl.BlockSpec((rows, d), lambda l, j: (0, 0)),
            pl.BlockSpec((None, d, ADA_TN), lambda l, j: (l, 0, j)),
            pl.BlockSpec((None, 1, ADA_TN), lambda l, j: (l, 0, j)),
        ],
        out_specs=pl.BlockSpec((None, rows, ADA_TN), lambda l, j: (l, 0, j)),
        out_shape=jax.ShapeDtypeStruct((depth, rows, n6), F32),
        compiler_params=_params(("arbitrary", "arbitrary")),
        name="ada",
    )(cvec, ada_w, ada_b.reshape(depth, 1, n6))


def _row_sources(x, tm, width, col_of):
    if not isinstance(x, tuple):
        return [pl.BlockSpec((tm, width), lambda i, *r: (i, col_of(i, *r)))], [x], None
    xa, xb = x
    na = xa.shape[0] // tm
    specs = [
        pl.BlockSpec((tm, width), lambda i, *r: (jnp.minimum(i, na - 1), col_of(i, *r))),
        pl.BlockSpec((tm, width), lambda i, *r: (jnp.maximum(i - na, 0), col_of(i, *r))),
    ]
    return specs, [xa, xb], na


def _read_rows(refs, split):
    if split is None:
        return refs[0][...]
    return jnp.where(pl.program_id(0) < split, refs[0][...], refs[1][...])


def _normmod_kernel(*refs, transpose, split):
    *x_refs, g_ref, sh_ref, sc_ref, o_ref = refs
    x = _read_rows(x_refs, split)
    y = x * lax.rsqrt(jnp.mean(x * x, axis=-1, keepdims=True) + NORM_EPS) * g_ref[...]
    h = y * (1.0 + sc_ref[...]) + sh_ref[...]
    o_ref[...] = (h.T if transpose else h).astype(BF16)


def _normmod(x, t_rows, gain, shift, scale, tiles_per_batch, n_batch, transpose):
    d = gain.shape[0]
    tm = ROW_TILE
    row_map = _mod_row_map(tiles_per_batch, n_batch)
    x_specs, x_args, split = _row_sources(x, tm, d, lambda i: 0)
    if transpose:
        out_spec = pl.BlockSpec((d, tm), lambda i: (0, i))
        out_shape = jax.ShapeDtypeStruct((d, t_rows), BF16)
    else:
        out_spec = pl.BlockSpec((tm, d), lambda i: (i, 0))
        out_shape = jax.ShapeDtypeStruct((t_rows, d), BF16)
    return pl.pallas_call(
        functools.partial(_normmod_kernel, transpose=transpose, split=split),
        grid=(t_rows // tm,),
        in_specs=x_specs + [
            pl.BlockSpec((1, d), lambda i: (0, 0)),
            pl.BlockSpec((None, 1, d), row_map),
            pl.BlockSpec((None, 1, d), row_map),
        ],
        out_specs=out_spec,
        out_shape=out_shape,
        compiler_params=_params(("arbitrary",)),
        name="normmod_t" if transpose else "normmod",
    )(*x_args, gain.reshape(1, d), shift, scale)


def _log_sigmoid(x):
    return jnp.minimum(x, 0.0) - jnp.log1p(jnp.exp(-jnp.abs(x)))


def _mm_kernel(a_ref, w_ref, o_ref):
    o_ref[...] = jnp.dot(a_ref[...], w_ref[...], preferred_element_type=F32).astype(o_ref.dtype)


def _mm_headnorm_kernel(a_ref, w_ref, g_ref, o_ref, *, normed_tiles, hd):
    acc = jnp.dot(a_ref[...], w_ref[...], preferred_element_type=F32)
    normed = pl.program_id(1) < normed_tiles
    for c in range(0, acc.shape[1], hd):
        x = acc[:, c:c + hd]
        inv = lax.rsqrt(jnp.mean(x * x, axis=-1, keepdims=True) + NORM_EPS)
        o_ref[:, c:c + hd] = (x * jnp.where(normed, inv, 1.0) * g_ref[...]).astype(o_ref.dtype)


def _mm_gates_kernel(a_ref, w_ref, b_ref, o_ref, *, heads):
    z = jnp.dot(a_ref[...], w_ref[...], preferred_element_type=F32) + b_ref[...]
    lane = lax.broadcasted_iota(jnp.int32, (1, z.shape[1]), 1)
    is_forget = ((lane // heads) % 2 == 1) & (lane < 4 * heads)
    o_ref[...] = jnp.where(is_forget, _log_sigmoid(z), z)


def _mm_resid_kernel(*refs, split):
    a_ref, w_ref, *r_refs, g_ref, o_ref = refs
    acc = jnp.dot(a_ref[...], w_ref[...], preferred_element_type=F32)
    o_ref[...] = _read_rows(r_refs, split) + g_ref[...] * acc


def _matmul(a, w, out_dtype, tn=MM_TN):
    m, k = a.shape
    n = w.shape[1]
    tn = min(tn, n)
    return pl.pallas_call(
        _mm_kernel,
        grid=(m // MM_TM, n // tn),
        in_specs=[
            pl.BlockSpec((MM_TM, k), lambda i, j: (i, 0)),
            pl.BlockSpec((k, tn), lambda i, j: (0, j)),
        ],
        out_specs=pl.BlockSpec((MM_TM, tn), lambda i, j: (i, j)),
        out_shape=jax.ShapeDtypeStruct((m, n), out_dtype),
        compiler_params=_params(("arbitrary", "arbitrary")),
        name="matmul",
    )(a, w)


def _matmul_gates(a, w, bias, heads):
    m, k = a.shape
    n = w.shape[1]
    return pl.pallas_call(
        functools.partial(_mm_gates_kernel, heads=heads),
        grid=(m // MM_TM,),
        in_specs=[
            pl.BlockSpec((MM_TM, k), lambda i: (i, 0)),
            pl.BlockSpec((k, n), lambda i: (0, 0)),
            pl.BlockSpec((1, n), lambda i: (0, 0)),
        ],
        out_specs=pl.BlockSpec((MM_TM, n), lambda i: (i, 0)),
        out_shape=jax.ShapeDtypeStruct((m, n), F32),
        compiler_params=_params(("arbitrary",)),
        name="matmul_gates",
    )(a, w, bias)


def _matmul_qkv(a, w, q_gain, k_gain, hd):
    m, k = a.shape
    n = w.shape[1]
    tn = MM_TN
    tiles = n // 3 // tn
    gains = jnp.stack([q_gain * hd ** -0.5, k_gain, jnp.ones_like(k_gain)]).reshape(3, 1, hd)
    return pl.pallas_call(
        functools.partial(_mm_headnorm_kernel, normed_tiles=2 * tiles, hd=hd),
        grid=(m // MM_TM, n // tn),
        in_specs=[
            pl.BlockSpec((MM_TM, k), lambda i, j: (i, 0)),
            pl.BlockSpec((k, tn), lambda i, j: (0, j)),
            pl.BlockSpec((None, 1, hd), lambda i, j: (j // tiles, 0, 0)),
        ],
        out_specs=pl.BlockSpec((MM_TM, tn), lambda i, j: (i, j)),
        out_shape=jax.ShapeDtypeStruct((m, n), BF16),
        compiler_params=_params(("arbitrary", "arbitrary")),
        name="matmul_qkv",
    )(a, w, gains)


def _matmul_resid(a, w, resid, gate, tiles_per_batch, n_batch):
    m, k = a.shape
    n = w.shape[1]
    tn = min(MM_TN // 2, n)
    row_map = _mod_row_map(tiles_per_batch, n_batch)
    r_specs, r_args, split = _row_sources(resid, MM_TM, tn, lambda i, j: j)
    return pl.pallas_call(
        functools.partial(_mm_resid_kernel, split=split),
        grid=(m // MM_TM, n // tn),
        in_specs=[
            pl.BlockSpec((MM_TM, k), lambda i, j: (i, 0)),
            pl.BlockSpec((k, tn), lambda i, j: (0, j)),
        ] + r_specs + [
            pl.BlockSpec((None, 1, tn), lambda i, j: row_map(i)[:2] + (j,)),
        ],
        out_specs=pl.BlockSpec((MM_TM, tn), lambda i, j: (i, j)),
        out_shape=jax.ShapeDtypeStruct((m, n), F32),
        compiler_params=_params(("arbitrary", "arbitrary")),
        name="matmul_resid",
    )(a, w, *r_args, gate)


def _split_dot(lhs, rhs, split_lhs):
    r = lhs if split_lhs else rhs
    acc = None
    for _ in range(3):
        p = r.astype(BF16)
        t = (jnp.dot(p, rhs, preferred_element_type=F32) if split_lhs
             else jnp.dot(lhs, p, preferred_element_type=F32))
        acc = t if acc is None else acc + t
        r = r - p.astype(F32)
    return acc


def _mlstm_kernel(*refs, reverse, finish, nctx, heads, dk, dv):
    if finish:
        (q_ref, k_ref, v_ref, gc_ref, gr_ref, cos_ref, sin_ref,
         hf_ref, og_ref, hg_ref, out_ref, ct_ref, m_ref) = refs
    else:
        (q_ref, k_ref, v_ref, gc_ref, gr_ref, cos_ref, sin_ref,
         out_ref, ct_ref, m_ref) = refs
    h = pl.program_id(1)
    c = pl.program_id(2)
    L = CHUNK

    @pl.when(c == 0)
    def _():
        ct_ref[...] = jnp.zeros_like(ct_ref)
        m_ref[...] = jnp.zeros_like(m_ref)

    is_lat = c >= nctx
    cs = jnp.where(is_lat, cos_ref[...], 1.0)
    sn = jnp.where(is_lat, sin_ref[...], 0.0)
    half = dk // 2

    def rope(x):
        x1, x2 = x[:, :half], x[:, half:]
        return jnp.concatenate([x1 * cs - x2 * sn, x2 * cs + x1 * sn], axis=-1)

    qb = (rope(q_ref[...].astype(F32)) * (dk ** -0.5)).astype(BF16)
    kb = rope(k_ref[...].astype(F32)).astype(BF16)

    direction = 1 if reverse else 0
    idx_i = (2 * direction) * heads + h
    idx_f = (2 * direction + 1) * heads + h
    lane = lax.broadcasted_iota(jnp.int32, (1, LANES), 1)
    gc = gc_ref[...]
    i_col = jnp.sum(jnp.where(lane == idx_i, gc, 0.0), axis=1, keepdims=True)
    lf_col = jnp.sum(jnp.where(lane == idx_f, gc, 0.0), axis=1, keepdims=True)
    sub = lax.broadcasted_iota(jnp.int32, (4 * heads, 1), 0)
    gr = gr_ref[...]
    i_row = jnp.sum(jnp.where(sub == idx_i, gr, 0.0), axis=0, keepdims=True)
    lf_row = jnp.sum(jnp.where(sub == idx_f, gr, 0.0), axis=0, keepdims=True)

    ti = lax.broadcasted_iota(jnp.int32, (L, L), 0)
    si = lax.broadcasted_iota(jnp.int32, (L, L), 1)
    mask = (si >= ti) if reverse else (si <= ti)
    mask_t = (ti >= si) if reverse else (ti <= si)
    tri = jnp.where(mask, 1.0, 0.0).astype(BF16)
    tri_t = jnp.where(mask_t, 1.0, 0.0).astype(BF16)
    b_col = _split_dot(tri, jnp.broadcast_to(lf_col, (L, LANES)), split_lhs=False)[:, 0:1]
    b_row = _split_dot(jnp.broadcast_to(lf_row, (2 * SUBLANES, L)), tri_t, split_lhs=True)[0:1]

    d = jnp.where(mask, b_col - b_row + i_row, -jnp.inf)
    m_prev = m_ref[0:1, 0:1]
    g = b_col + m_prev
    mt = jnp.maximum(g, jnp.max(d, axis=1, keepdims=True))
    s = lax.dot_general(qb, kb, (((1,), (1,)), ((), ())), preferred_element_type=F32)
    s = s * jnp.exp(d - mt)
    inter = jnp.exp(g - mt)

    v_aug = jnp.concatenate([v_ref[...], jnp.ones((L, LANES), BF16)], axis=1)
    nd = (jnp.dot(s.astype(BF16), v_aug, preferred_element_type=F32)
          + inter * jnp.dot(qb, ct_ref[...].astype(BF16), preferred_element_type=F32))
    den = nd[:, dv:dv + 1]
    hout = nd[:, :dv] * (1.0 / jnp.maximum(jnp.abs(den), jnp.exp(-mt)))

    bl = b_col[0:1] if reverse else b_col[L - 1:L]
    wl = bl - b_col + i_col
    m_new = jnp.maximum(bl + m_prev, jnp.max(wl, axis=0, keepdims=True))
    a = jnp.exp(bl + m_prev - m_new)
    w = jnp.exp(wl - m_new)
    vw = (v_aug.astype(F32) * w).astype(BF16)
    ct_ref[...] = a * ct_ref[...] + lax.dot_general(
        kb, vw, (((0,), (0,)), ((), ())), preferred_element_type=F32)
    m_ref[...] = jnp.broadcast_to(m_new, m_ref.shape)

    if finish:
        hs = hf_ref[...] + hout
        y = hs * lax.rsqrt(jnp.mean(hs * hs, axis=-1, keepdims=True) + NORM_EPS) * hg_ref[...]
        out_ref[...] = (y * jax.nn.sigmoid(og_ref[...].astype(F32))).astype(out_ref.dtype)
    else:
        out_ref[...] = hout


def _mlstm_pass(proj, gates_c, gates_r, cos, sin, n_batch, n_lat, n_ctx,
                reverse, h_fwd=None, head_gain=None):
    t_all = proj.shape[0]
    heads = MLSTM_HEADS
    d = proj.shape[1] // 3
    dk, dv = d // 16, d // 8
    nlat, nctx = n_lat // CHUNK, n_ctx // CHUNK
    finish = h_fwd is not None

    def blk(b, c):
        cc = (nctx - 1 - c) if reverse else c
        lc = (nlat - 1 - (c - nctx)) if reverse else (c - nctx)
        return jnp.where(c < nctx, n_batch * nlat + b * nctx + cc, b * nlat + lc)

    def lat_blk(c):
        lc = (nlat - 1 - (c - nctx)) if reverse else (c - nctx)
        return jnp.clip(lc, 0, nlat - 1)

    in_specs = [
        pl.BlockSpec((CHUNK, dk), lambda b, h, c: (blk(b, c), h)),
        pl.BlockSpec((CHUNK, dk), lambda b, h, c: (blk(b, c), heads + h)),
        pl.BlockSpec((CHUNK, dv), lambda b, h, c: (blk(b, c), heads + h)),
        pl.BlockSpec((CHUNK, LANES), lambda b, h, c: (blk(b, c), 0)),
        pl.BlockSpec((4 * heads, CHUNK), lambda b, h, c: (0, blk(b, c))),
        pl.BlockSpec((CHUNK, dk // 2), lambda b, h, c: (lat_blk(c), 0)),
        pl.BlockSpec((CHUNK, dk // 2), lambda b, h, c: (lat_blk(c), 0)),
    ]
    args = [proj, proj, proj, gates_c, gates_r, cos, sin]
    if finish:
        in_specs += [
            pl.BlockSpec((CHUNK, dv), lambda b, h, c: (blk(b, c), h)),
            pl.BlockSpec((CHUNK, dv), lambda b, h, c: (blk(b, c), 2 * heads + h)),
            pl.BlockSpec((None, 1, dv), lambda b, h, c: (h, 0, 0)),
        ]
        args += [h_fwd, proj, head_gain.reshape(heads, 1, dv)]
    return pl.pallas_call(
        functools.partial(_mlstm_kernel, reverse=reverse, finish=finish, nctx=nctx,
                          heads=heads, dk=dk, dv=dv),
        grid=(n_batch, heads, nctx + nlat),
        in_specs=in_specs,
        out_specs=pl.BlockSpec((CHUNK, dv), lambda b, h, c: (blk(b, c), h)),
        out_shape=jax.ShapeDtypeStruct((t_all, heads * dv), BF16 if finish else F32),
        scratch_shapes=[pltpu.VMEM((dk, dv + LANES), F32), pltpu.VMEM((SUBLANES, LANES), F32)],
        compiler_params=_params(("arbitrary", "arbitrary", "arbitrary")),
        name="mlstm_bwd_finish" if finish else "mlstm_fwd",
    )(*args)


def _na_bias_blocks(rpb, rows):
    qr, span = NA_QROWS, 3 * NA_QROWS
    nh = rpb.shape[0]
    n_dr = 2 * NA_KH - 1
    c = np.arange(GRID_W)[:, None]
    cp = np.arange(GRID_W)[None, :]
    cs = np.clip(c - NA_KW // 2, 0, GRID_W - NA_KW)
    valid_c = (cp >= cs) & (cp < cs + NA_KW)
    onehot = ((cp - c + NA_KW - 1)[..., None] == np.arange(2 * NA_KW - 1)) & valid_c[..., None]
    colmat = jnp.einsum("hrd,cpd->hrcp", rpb, jnp.asarray(onehot, F32), precision=lax.Precision.HIGHEST)
    colmat = jnp.where(valid_c, colmat, -jnp.inf)
    masked = jnp.full((nh, GRID_W, GRID_W), -jnp.inf, F32)
    out = []
    for r0, u0 in ((0, 0), (qr, 0), (rows - qr, rows - span)):
        row_blocks = []
        for rl in range(qr):
            r = r0 + rl
            rs = min(max(r - NA_KH // 2, 0), rows - NA_KH)
            blocks = []
            for i in range(span):
                ia = u0 + i
                dr = ia - r + NA_KH - 1
                blocks.append(colmat[:, dr] if rs <= ia < rs + NA_KH and 0 <= dr < n_dr else masked)
            row_blocks.append(jnp.concatenate(blocks, axis=-1))
        out.append(jnp.concatenate(row_blocks, axis=-2))
    return jnp.stack(out, axis=0)


def _na_kernel(q_ref, k0_ref, k1_ref, k2_ref, v0_ref, v1_ref, v2_ref, kc_ref, vc_ref, bias_ref, o_ref):
    hd = NA_HEAD_DIM
    nt = (((1,), (1,)), ((), ()))
    ones = jnp.ones((CHUNK, hd), BF16)
    for hh in range(NA_HEAD_BLOCK):
        lanes = slice(hh * hd, (hh + 1) * hd)
        q = q_ref[:, lanes]
        scores = [lax.dot_general(q, k_ref[:, lanes], nt, preferred_element_type=F32)
                  + bias_ref[hh, :, j * CHUNK:(j + 1) * CHUNK]
                  for j, k_ref in enumerate((k0_ref, k1_ref, k2_ref))]
        scores.append(lax.dot_general(q, kc_ref[:, lanes], nt, preferred_element_type=F32))
        m = jnp.maximum(jnp.maximum(scores[0], scores[1]), jnp.maximum(scores[2], scores[3]))
        m = m.max(axis=-1, keepdims=True)
        acc = None
        for s, v_ref in zip(scores, (v0_ref, v1_ref, v2_ref, vc_ref)):
            v_aug = jnp.concatenate([v_ref[:, lanes], ones], axis=1)
            pv = jnp.dot(jnp.exp(s - m).astype(BF16), v_aug, preferred_element_type=F32)
            acc = pv if acc is None else acc + pv
        o_ref[:, lanes] = (acc[:, :hd] * (1.0 / acc[:, hd:hd + 1])).astype(o_ref.dtype)


def _na_attention(qkv, bias, n_batch, n_lat, n_ctx):
    d = qkv.shape[1] // 3
    hb = NA_HEAD_BLOCK
    width = hb * NA_HEAD_DIM
    nhb = d // width
    nrb = n_lat // CHUNK
    assert n_ctx == CHUNK and nrb >= 3

    def kv_map(off, j):
        return lambda h, b, r: (b * nrb + jnp.clip(r - 1, 0, nrb - 3) + j, off * nhb + h)

    def variant(r):
        return jnp.where(r == 0, 0, jnp.where(r == nrb - 1, 2, 1))

    blk = (CHUNK, width)
    in_specs = [pl.BlockSpec(blk, lambda h, b, r: (b * nrb + r, h))]
    in_specs += [pl.BlockSpec(blk, kv_map(1, j)) for j in range(3)]
    in_specs += [pl.BlockSpec(blk, kv_map(2, j)) for j in range(3)]
    in_specs += [
        pl.BlockSpec(blk, lambda h, b, r: (n_batch * nrb + b, nhb + h)),
        pl.BlockSpec(blk, lambda h, b, r: (n_batch * nrb + b, 2 * nhb + h)),
        pl.BlockSpec((None, hb, CHUNK, 3 * CHUNK), lambda h, b, r: (variant(r), h, 0, 0)),
    ]
    return pl.pallas_call(
        _na_kernel,
        grid=(nhb, n_batch, nrb),
        in_specs=in_specs,
        out_specs=pl.BlockSpec(blk, lambda h, b, r: (b * nrb + r, h)),
        out_shape=jax.ShapeDtypeStruct((n_batch * n_lat, d), BF16),
        compiler_params=_params(("arbitrary", "arbitrary", "arbitrary")),
        name="na_attention",
    )(*([qkv] * 9), bias)


def _sort16_pairs():
    pairs = []

    def merge(lo, n, r):
        step = r * 2
        if step < n:
            merge(lo, n, step)
            merge(lo + r, n, step)
            pairs.extend((i, i + r) for i in range(lo + r, lo + n - r, step))
        else:
            pairs.append((lo, lo + r))

    def sort(lo, n):
        if n > 1:
            sort(lo, n // 2)
            sort(lo + n // 2, n // 2)
            merge(lo, n, 1)

    sort(0, PEER_TOPK)
    return pairs


_SORT16 = _sort16_pairs()


def _bitonic_merge16(x):
    x = list(x)
    d = PEER_TOPK // 2
    while d:
        for i in range(PEER_TOPK):
            if i & d == 0:
                x[i], x[i + d] = jnp.maximum(x[i], x[i + d]), jnp.minimum(x[i], x[i + d])
        d //= 2
    return x


def _merge_top16(top, other):
    n = len(other)
    merged = list(top[:PEER_TOPK - n])
    merged += [jnp.maximum(top[i], other[PEER_TOPK - 1 - i]) for i in range(PEER_TOPK - n, PEER_TOPK)]
    return _bitonic_merge16(merged)


def _top16_sorted(x):
    rows = [x[SUBLANES * r:SUBLANES * (r + 1)] for r in range(PEER_TOPK)]
    for i, j in _SORT16:
        rows[i], rows[j] = jnp.maximum(rows[i], rows[j]), jnp.minimum(rows[i], rows[j])
    shift = SUBLANES // 2
    while shift:
        rolled = [pltpu.roll(v, shift, 0) for v in rows]
        rows = _merge_top16(rows, rolled)
        shift //= 2
    return rows


def _top16_pair_sums(a, b):
    top = [a[0] + b[j] for j in range(PEER_TOPK)]
    for i in range(1, SUBLANES):
        top = _merge_top16(top, [a[i] + b[j] for j in range(PEER_TOPK // (i + 1))])
    return _merge_top16(top, [a[i] + b[0] for i in range(SUBLANES, PEER_TOPK)])


def _peer_select_kernel(ht_ref, wq_ref, k1_ref, k2_ref, s1_ref, s2_ref, st_ref, qt_ref):
    heads = s1_ref.shape[0]
    tm = ht_ref.shape[1]
    dk = qt_ref.shape[0] // heads
    half = dk // 2
    nlb = tm // LANES
    r8 = lax.broadcasted_iota(jnp.int32, (SUBLANES, LANES), 0) % nlb
    qt_ref[...] = jnp.dot(wq_ref[...], ht_ref[...], preferred_element_type=F32)

    def head(h, carry):
        row = pl.multiple_of(h * dk, dk)
        q1 = qt_ref[pl.ds(row, half), :].astype(BF16)
        q2 = qt_ref[pl.ds(row + half, half), :].astype(BF16)
        s1 = jnp.dot(k1_ref[...], q1, preferred_element_type=F32)
        s2 = jnp.dot(k2_ref[...], q2, preferred_element_type=F32)
        tops1, tops2 = [], []
        for lb in range(nlb):
            sl = slice(lb * LANES, (lb + 1) * LANES)
            tops1.append(_top16_sorted(s1[:, sl]))
            tops2.append(_top16_sorted(s2[:, sl]))

        def pack(tops, r):
            out = tops[nlb - 1][r]
            for lb in range(nlb - 2, -1, -1):
                out = jnp.where(r8 == lb, tops[lb][r], out)
            return out

        a = [pack(tops1, r) for r in range(PEER_TOPK)]
        b = [pack(tops2, r) for r in range(PEER_TOPK)]
        a = [v - a[0] for v in a]
        b = [v - b[0] for v in b]
        top = _top16_pair_sums(a, b)
        z = jnp.exp(top[0])
        for v in top[1:]:
            z = z + jnp.exp(v)
        log_z = jnp.log(z)
        b = [v - log_z for v in b]
        st_ref[h] = _top16_pair_sums(a, b)[PEER_TOPK - 1]
        for lb in range(nlb):
            sl = slice(lb * LANES, (lb + 1) * LANES)
            s1_ref[h, :, sl] = s1[:, sl] - tops1[lb][0][0:1]
            s2_ref[h, :, sl] = (s2[:, sl] - tops2[lb][0][0:1]) - log_z[lb:lb + 1]
        return carry

    lax.fori_loop(0, heads, head, 0)


def _peer_select(ht, wq_t, keys1, keys2):
    d, t = ht.shape
    heads = PEER_HEADS
    dk = wq_t.shape[0] // heads
    nk = keys1.shape[0]
    tm = PEER_TM
    assert nk == PEER_TOPK * SUBLANES and tm // LANES <= SUBLANES
    out_s = jax.ShapeDtypeStruct((heads, nk, t), F32)
    return pl.pallas_call(
        _peer_select_kernel,
        grid=(t // tm,),
        in_specs=[
            pl.BlockSpec((d, tm), lambda i: (0, i)),
            pl.BlockSpec((heads * dk, d), lambda i: (0, 0)),
            pl.BlockSpec((nk, dk // 2), lambda i: (0, 0)),
            pl.BlockSpec((nk, dk // 2), lambda i: (0, 0)),
        ],
        out_specs=[
            pl.BlockSpec((heads, nk, tm), lambda i: (0, 0, i)),
            pl.BlockSpec((heads, nk, tm), lambda i: (0, 0, i)),
            pl.BlockSpec((heads, None, SUBLANES, LANES), lambda i: (0, i, 0, 0)),
        ],
        out_shape=[out_s, out_s, jax.ShapeDtypeStruct((heads, t // tm, SUBLANES, LANES), F32)],
        scratch_shapes=[pltpu.VMEM((heads * dk, tm), F32)],
        compiler_params=_params(("arbitrary",)),
        name="peer_select",
    )(ht, wq_t, keys1, keys2)


def _gelu(x):
    return 0.5 * x * (1.0 + lax.erf(x * np.float32(1.0 / np.sqrt(2.0))))


def _peer_dense_kernel(ht_ref, u_ref, vt_ref, s1_ref, s2_ref, st_ref, o_ref, hid0_ref, hid1_ref, act_ref):
    j = pl.program_id(1)
    heads, nk, tm = s2_ref.shape
    te = u_ref.shape[0]
    groups = te // nk
    assert 2 * groups == s1_ref.shape[1]

    last = pl.num_programs(1) - 1

    def up_project(hid_w_ref):
        hid_w_ref[...] = jnp.dot(u_ref[...], ht_ref[...], preferred_element_type=F32)

    def step(hid_w_ref, hid_r_ref, k1_base):
        if hid_w_ref is not None:
            up_project(hid_w_ref)
        for lb in range(tm // LANES):
            sl = slice(lb * LANES, (lb + 1) * LANES)
            for kk in range(groups):
                k1 = k1_base + kk
                for r0 in range(0, nk, PEER_SUB):
                    gate = jnp.zeros((PEER_SUB, LANES), F32)
                    for h in range(heads):
                        s = s1_ref[h, k1:k1 + 1, sl] + s2_ref[h, r0:r0 + PEER_SUB, sl]
                        gate = gate + jnp.where(s >= st_ref[h, lb:lb + 1, :], jnp.exp(s), 0.0)
                    rows = slice(kk * nk + r0, kk * nk + r0 + PEER_SUB)
                    act_ref[rows, sl] = (gate * _gelu(hid_r_ref[rows, sl])).astype(BF16)
        o_ref[...] += jnp.dot(vt_ref[...], act_ref[...], preferred_element_type=F32)

    @pl.when(j == 0)
    def _():
        o_ref[...] = jnp.zeros_like(o_ref)
        up_project(hid0_ref)

    @pl.when((j % 2 == 0) & (j > 0) & (j < last))
    def _():
        step(hid0_ref, hid1_ref, groups)

    @pl.when(j % 2 == 1)
    def _():
        step(hid1_ref, hid0_ref, 0)

    @pl.when(j == last)
    def _():
        step(None, hid1_ref, groups)


def _peer_dense(ht, u, vt, s1, s2, stats):
    d, t = ht.shape
    n_exp = u.shape[0]
    heads, nk, _ = s1.shape
    tm, te = PEER_TM, PEER_TE
    nblk = n_exp // te
    assert nblk % 2 == 0
    return pl.pallas_call(
        _peer_dense_kernel,
        grid=(t // tm, nblk + 1),
        in_specs=[
            pl.BlockSpec((d, tm), lambda i, j: (0, i)),
            pl.BlockSpec((te, d), lambda i, j: (jnp.minimum(j, nblk - 1), 0)),
            pl.BlockSpec((d, te), lambda i, j: (0, jnp.maximum(j - 1, 0))),
            pl.BlockSpec((heads, SUBLANES, tm), lambda i, j: (0, jnp.maximum(j - 1, 0) // 2, i)),
            pl.BlockSpec((heads, nk, tm), lambda i, j: (0, 0, i)),
            pl.BlockSpec((heads, None, SUBLANES, LANES), lambda i, j: (0, i, 0, 0)),
        ],
        out_specs=pl.BlockSpec((d, tm), lambda i, j: (0, i)),
        out_shape=jax.ShapeDtypeStruct((d, t), F32),
        scratch_shapes=[pltpu.VMEM((te, tm), F32), pltpu.VMEM((te, tm), F32), pltpu.VMEM((te, tm), BF16)],
        compiler_params=_params(("arbitrary", "arbitrary")),
        name="peer_dense",
    )(ht, u, vt, s1, s2, stats)


def _resid_t_kernel(x_ref, yt_ref, g_ref, o_ref):
    o_ref[...] = x_ref[...] + g_ref[...] * yt_ref[...].T


def _resid_t(x, yt, gate, tiles_per_batch, n_batch):
    d, t = yt.shape
    tm = ROW_TILE
    return pl.pallas_call(
        _resid_t_kernel,
        grid=(t // tm,),
        in_specs=[
            pl.BlockSpec((tm, d), lambda i: (i, 0)),
            pl.BlockSpec((d, tm), lambda i: (0, i)),
            pl.BlockSpec((None, 1, d), _mod_row_map(tiles_per_batch, n_batch)),
        ],
        out_specs=pl.BlockSpec((tm, d), lambda i: (i, 0)),
        out_shape=jax.ShapeDtypeStruct((t, d), F32),
        compiler_params=_params(("arbitrary",)),
        name="resid_t",
    )(x, yt, gate)


def _peer_ffn(x, t_rows, gain, mods, w_query, sub_keys, u, v, n_lat, n_batch):
    shift, scale, gate = mods
    ht = _normmod(x, t_rows, gain, shift, scale, n_lat // ROW_TILE, n_batch, transpose=True)
    s1, s2, stats = _peer_select(ht, w_query.T.astype(BF16), sub_keys[0].astype(BF16),
                                 sub_keys[1].astype(BF16))
    yt = _peer_dense(ht, u.astype(BF16), v.T.astype(BF16), s1, s2, stats)
    return _resid_t(x, yt, gate, n_lat // ROW_TILE, n_batch)


def _axial_rope(n, head_dim):
    quarter = head_dim // 4
    inv = ROPE_THETA ** (-jnp.arange(quarter, dtype=F32) / quarter)
    t = jnp.arange(n, dtype=jnp.int32)
    row = (t // GRID_W).astype(F32)
    col = (t % GRID_W).astype(F32)
    ang = jnp.concatenate([row[:, None] * inv, col[:, None] * inv], axis=-1)
    return jnp.cos(ang), jnp.sin(ang)


def kernel(x, c, ctx, c_ctx, ada_w, ada_b, norm_mix, norm_ffn, mlstm_w_in, mlstm_gate_b, mlstm_head_gain, mlstm_w_out, na_w_qkv, na_q_gain, na_k_gain, na_rpb, na_w_out, peer_w_query, peer_sub_keys, peer_u, peer_v):
    n_batch, n_lat, d = x.shape
    n_ctx = ctx.shape[1]
    depth = ada_w.shape[0]
    t_lat = n_batch * n_lat
    t_all = t_lat + n_batch * n_ctx
    heads = MLSTM_HEADS
    assert depth == 2 and n_lat % MM_TM == 0 and t_all % MM_TM == 0 and n_ctx % CHUNK == 0

    mod_rows = 2 * SUBLANES
    cvec = jnp.concatenate(
        [c, c_ctx[None, :], jnp.zeros((mod_rows - n_batch - 1, d), F32)], axis=0)
    mod = _ada(cvec, ada_w, ada_b).reshape(depth, mod_rows, N_ADA, d)

    def mods(layer, k):
        return mod[layer, :, k, :].reshape(mod_rows, 1, d)

    xs = (x.reshape(t_lat, d), ctx.reshape(n_batch * n_ctx, d))

    hmix = _normmod(xs, t_all, norm_mix[0], mods(0, 0), mods(0, 1), n_lat // ROW_TILE, n_batch,
                    transpose=False)
    w_in = mlstm_w_in[0]
    proj = _matmul(hmix, w_in[:, :3 * d].astype(BF16), BF16)
    w_gate = jnp.pad(w_in[:, 3 * d:], ((0, 0), (0, LANES - 4 * heads))).astype(BF16)
    gate_b = jnp.pad(mlstm_gate_b[0], (0, LANES - 4 * heads)).reshape(1, LANES)
    gates_c = _matmul_gates(hmix, w_gate, gate_b, heads)
    gates_r = gates_c[:, :4 * heads].T
    cos, sin = _axial_rope(n_lat, d // 16)
    scan_args = (proj, gates_c, gates_r, cos, sin, n_batch, n_lat, n_ctx)
    h_fwd = _mlstm_pass(*scan_args, reverse=False)
    ymix = _mlstm_pass(*scan_args, reverse=True, h_fwd=h_fwd, head_gain=mlstm_head_gain[0])
    xs = _matmul_resid(ymix, mlstm_w_out[0].astype(BF16), xs, mods(0, 2), n_lat // MM_TM, n_batch)
    xs = _peer_ffn(xs, t_all, norm_ffn[0], (mods(0, 3), mods(0, 4), mods(0, 5)),
                   peer_w_query[0], peer_sub_keys[0], peer_u[0], peer_v[0], n_lat, n_batch)

    hmix = _normmod(xs, t_all, norm_mix[1], mods(1, 0), mods(1, 1), n_lat // ROW_TILE, n_batch,
                    transpose=False)
    qkv = _matmul_qkv(hmix, na_w_qkv[0].astype(BF16), na_q_gain[0], na_k_gain[0], NA_HEAD_DIM)
    bias = _na_bias_blocks(na_rpb[0], n_lat // GRID_W)
    att = _na_attention(qkv, bias, n_batch, n_lat, n_ctx)
    xs = _matmul_resid(att, na_w_out[0].astype(BF16), xs, mods(1, 2), n_lat // MM_TM, n_batch)
    xs = _peer_ffn(xs, t_lat, norm_ffn[1], (mods(1, 3), mods(1, 4), mods(1, 5)),
                   peer_w_query[1], peer_sub_keys[1], peer_u[1], peer_v[1], n_lat, n_batch)
    return xs.reshape(n_batch, n_lat, d)
```

```python
import functools

import numpy as np
import jax
import jax.numpy as jnp
from jax import lax
from jax.experimental import pallas as pl
from jax.experimental.pallas import tpu as pltpu

F32 = jnp.float32
BF16 = jnp.bfloat16

NORM_EPS = 1e-6
GRID_W = 64
N_ADA = 6
MLSTM_HEADS = 8
ROPE_THETA = 10000.0
NA_HEAD_DIM = 128
NA_KH = 8
NA_KW = 16
PEER_HEADS = 8
PEER_TOPK = 16

LANES = 128
SUBLANES = 8
CHUNK = 256
NA_QROWS = CHUNK // GRID_W
NA_HEAD_BLOCK = 4
ROW_TILE = 256
MM_TM = 512
MM_TN = 2048
ADA_TN = 512
PEER_TM = 512
PEER_TE = 512
PEER_SUB = 32
VMEM_LIMIT = 60 * 1024 * 1024


def _params(sem):
    return pltpu.CompilerParams(dimension_semantics=sem, vmem_limit_bytes=VMEM_LIMIT)


def _mod_row_map(tiles_per_batch, n_batch):
    return lambda i, *_: (jnp.minimum(i // tiles_per_batch, n_batch), 0, 0)


def _ada_kernel(c_ref, w_ref, b_ref, o_ref):
    c = c_ref[...]
    s = (c * jax.nn.sigmoid(c)).astype(BF16)
    o_ref[...] = jnp.dot(s, w_ref[...].astype(BF16), preferred_element_type=F32) + b_ref[...]


def _ada(cvec, ada_w, ada_b):
    depth, d, n6 = ada_w.shape
    rows = cvec.shape[0]
    return pl.pallas_call(
        _ada_kernel,
        grid=(depth, n6 // ADA_TN),
        in_specs=[
            pl.BlockSpec((rows, d), lambda l, j: (0, 0)),
            pl.BlockSpec((None, d, ADA_TN), lambda l, j: (l, 0, j)),
            pl.BlockSpec((None, 1, ADA_TN), lambda l, j: (l, 0, j)),
        ],
        out_specs=pl.BlockSpec((None, rows, ADA_TN), lambda l, j: (l, 0, j)),
        out_shape=jax.ShapeDtypeStruct((depth, rows, n6), F32),
        compiler_params=_params(("arbitrary", "arbitrary")),
        name="ada",
    )(cvec, ada_w, ada_b.reshape(depth, 1, n6))


def _row_sources(x, tm, width, col_of):
    if not isinstance(x, tuple):
        return [pl.BlockSpec((tm, width), lambda i, *r: (i, col_of(i, *r)))], [x], None
    xa, xb = x
    na = xa.shape[0] // tm
    specs = [
        pl.BlockSpec((tm, width), lambda i, *r: (jnp.minimum(i, na - 1), col_of(i, *r))),
        pl.BlockSpec((tm, width), lambda i, *r: (jnp.maximum(i - na, 0), col_of(i, *r))),
    ]
    return specs, [xa, xb], na


def _read_rows(refs, split):
    if split is None:
        return refs[0][...]
    return jnp.where(pl.program_id(0) < split, refs[0][...], refs[1][...])


def _normmod_kernel(*refs, transpose, split):
    *x_refs, g_ref, sh_ref, sc_ref, o_ref = refs
    x = _read_rows(x_refs, split)
    y = x * lax.rsqrt(jnp.mean(x * x, axis=-1, keepdims=True) + NORM_EPS) * g_ref[...]
    h = y * (1.0 + sc_ref[...]) + sh_ref[...]
    o_ref[...] = (h.T if transpose else h).astype(BF16)


def _normmod(x, t_rows, gain, shift, scale, tiles_per_batch, n_batch, transpose):
    d = gain.shape[0]
    tm = ROW_TILE
    row_map = _mod_row_map(tiles_per_batch, n_batch)
    x_specs, x_args, split = _row_sources(x, tm, d, lambda i: 0)
    if transpose:
        out_spec = pl.BlockSpec((d, tm), lambda i: (0, i))
        out_shape = jax.ShapeDtypeStruct((d, t_rows), BF16)
    else:
        out_spec = pl.BlockSpec((tm, d), lambda i: (i, 0))
        out_shape = jax.ShapeDtypeStruct((t_rows, d), BF16)
    return pl.pallas_call(
        functools.partial(_normmod_kernel, transpose=transpose, split=split),
        grid=(t_rows // tm,),
        in_specs=x_specs + [
            pl.BlockSpec((1, d), lambda i: (0, 0)),
            pl.BlockSpec((None, 1, d), row_map),
            pl.BlockSpec((None, 1, d), row_map),
        ],
        out_specs=out_spec,
        out_shape=out_shape,
        compiler_params=_params(("arbitrary",)),
        name="normmod_t" if transpose else "normmod",
    )(*x_args, gain.reshape(1, d), shift, scale)


def _log_sigmoid(x):
    return jnp.minimum(x, 0.0) - jnp.log1p(jnp.exp(-jnp.abs(x)))


def _mm_kernel(a_ref, w_ref, o_ref):
    o_ref[...] = jnp.dot(a_ref[...], w_ref[...], preferred_element_type=F32).astype(o_ref.dtype)


def _mm_headnorm_kernel(a_ref, w_ref, g_ref, o_ref, *, normed_tiles, hd):
    acc = jnp.dot(a_ref[...], w_ref[...], preferred_element_type=F32)
    normed = pl.program_id(1) < normed_tiles
    for c in range(0, acc.shape[1], hd):
        x = acc[:, c:c + hd]
        inv = lax.rsqrt(jnp.mean(x * x, axis=-1, keepdims=True) + NORM_EPS)
        o_ref[:, c:c + hd] = (x * jnp.where(normed, inv, 1.0) * g_ref[...]).astype(o_ref.dtype)


def _mm_gates_kernel(a_ref, w_ref, b_ref, o_ref, *, heads):
    z = jnp.dot(a_ref[...], w_ref[...], preferred_element_type=F32) + b_ref[...]
    lane = lax.broadcasted_iota(jnp.int32, (1, z.shape[1]), 1)
    is_forget = ((lane // heads) % 2 == 1) & (lane < 4 * heads)
    o_ref[...] = jnp.where(is_forget, _log_sigmoid(z), z)


def _mm_resid_kernel(*refs, split):
    a_ref, w_ref, *r_refs, g_ref, o_ref = refs
    acc = jnp.dot(a_ref[...], w_ref[...], preferred_element_type=F32)
    o_ref[...] = _read_rows(r_refs, split) + g_ref[...] * acc


def _matmul(a, w, out_dtype, tn=MM_TN):
    m, k = a.shape
    n = w.shape[1]
    tn = min(tn, n)
    return pl.pallas_call(
        _mm_kernel,
        grid=(m // MM_TM, n // tn),
        in_specs=[
            pl.BlockSpec((MM_TM, k), lambda i, j: (i, 0)),
            pl.BlockSpec((k, tn), lambda i, j: (0, j)),
        ],
        out_specs=pl.BlockSpec((MM_TM, tn), lambda i, j: (i, j)),
        out_shape=jax.ShapeDtypeStruct((m, n), out_dtype),
        compiler_params=_params(("arbitrary", "arbitrary")),
        name="matmul",
    )(a, w)


def _matmul_gates(a, w, bias, heads):
    m, k = a.shape
    n = w.shape[1]
    return pl.pallas_call(
        functools.partial(_mm_gates_kernel, heads=heads),
        grid=(m // MM_TM,),
        in_specs=[
            pl.BlockSpec((MM_TM, k), lambda i: (i, 0)),
            pl.BlockSpec((k, n), lambda i: (0, 0)),
            pl.BlockSpec((1, n), lambda i: (0, 0)),
        ],
        out_specs=pl.BlockSpec((MM_TM, n), lambda i: (i, 0)),
        out_shape=jax.ShapeDtypeStruct((m, n), F32),
        compiler_params=_params(("arbitrary",)),
        name="matmul_gates",
    )(a, w, bias)


def _matmul_qkv(a, w, q_gain, k_gain, hd):
    m, k = a.shape
    n = w.shape[1]
    tn = MM_TN
    tiles = n // 3 // tn
    gains = jnp.stack([q_gain * hd ** -0.5, k_gain, jnp.ones_like(k_gain)]).reshape(3, 1, hd)
    return pl.pallas_call(
        functools.partial(_mm_headnorm_kernel, normed_tiles=2 * tiles, hd=hd),
        grid=(m // MM_TM, n // tn),
        in_specs=[
            pl.BlockSpec((MM_TM, k), lambda i, j: (i, 0)),
            pl.BlockSpec((k, tn), lambda i, j: (0, j)),
            pl.BlockSpec((None, 1, hd), lambda i, j: (j // tiles, 0, 0)),
        ],
        out_specs=pl.BlockSpec((MM_TM, tn), lambda i, j: (i, j)),
        out_shape=jax.ShapeDtypeStruct((m, n), BF16),
        compiler_params=_params(("arbitrary", "arbitrary")),
        name="matmul_qkv",
    )(a, w, gains)


def _matmul_resid(a, w, resid, gate, tiles_per_batch, n_batch):
    m, k = a.shape
    n = w.shape[1]
    tn = min(MM_TN // 2, n)
    row_map = _mod_row_map(tiles_per_batch, n_batch)
    r_specs, r_args, split = _row_sources(resid, MM_TM, tn, lambda i, j: j)
    return pl.pallas_call(
        functools.partial(_mm_resid_kernel, split=split),
        grid=(m // MM_TM, n // tn),
        in_specs=[
            pl.BlockSpec((MM_TM, k), lambda i, j: (i, 0)),
            pl.BlockSpec((k, tn), lambda i, j: (0, j)),
        ] + r_specs + [
            pl.BlockSpec((None, 1, tn), lambda i, j: row_map(i)[:2] + (j,)),
        ],
        out_specs=pl.BlockSpec((MM_TM, tn), lambda i, j: (i, j)),
        out_shape=jax.ShapeDtypeStruct((m, n), F32),
        compiler_params=_params(("arbitrary", "arbitrary")),
        name="matmul_resid",
    )(a, w, *r_args, gate)


def _split_dot(lhs, rhs, split_lhs):
    r = lhs if split_lhs else rhs
    acc = None
    for _ in range(3):
        p = r.astype(BF16)
        t = (jnp.dot(p, rhs, preferred_element_type=F32) if split_lhs
             else jnp.dot(lhs, p, preferred_element_type=F32))
        acc = t if acc is None else acc + t
        r = r - p.astype(F32)
    return acc


def _mlstm_kernel(*refs, reverse, finish, nctx, heads, dk, dv):
    if finish:
        (q_ref, k_ref, v_ref, gc_ref, gr_ref, cos_ref, sin_ref,
         hf_ref, og_ref, hg_ref, out_ref, ct_ref, m_ref) = refs
    else:
        (q_ref, k_ref, v_ref, gc_ref, gr_ref, cos_ref, sin_ref,
         out_ref, ct_ref, m_ref) = refs
    h = pl.program_id(1)
    c = pl.program_id(2)
    L = CHUNK

    @pl.when(c == 0)
    def _():
        ct_ref[...] = jnp.zeros_like(ct_ref)
        m_ref[...] = jnp.zeros_like(m_ref)

    is_lat = c >= nctx
    cs = jnp.where(is_lat, cos_ref[...], 1.0)
    sn = jnp.where(is_lat, sin_ref[...], 0.0)
    half = dk // 2

    def rope(x):
        x1, x2 = x[:, :half], x[:, half:]
        return jnp.concatenate([x1 * cs - x2 * sn, x2 * cs + x1 * sn], axis=-1)

    qb = (rope(q_ref[...].astype(F32)) * (dk ** -0.5)).astype(BF16)
    kb = rope(k_ref[...].astype(F32)).astype(BF16)

    direction = 1 if reverse else 0
    idx_i = (2 * direction) * heads + h
    idx_f = (2 * direction + 1) * heads + h
    lane = lax.broadcasted_iota(jnp.int32, (1, LANES), 1)
    gc = gc_ref[...]
    i_col = jnp.sum(jnp.where(lane == idx_i, gc, 0.0), axis=1, keepdims=True)
    lf_col = jnp.sum(jnp.where(lane == idx_f, gc, 0.0), axis=1, keepdims=True)
    sub = lax.broadcasted_iota(jnp.int32, (4 * heads, 1), 0)
    gr = gr_ref[...]
    i_row = jnp.sum(jnp.where(sub == idx_i, gr, 0.0), axis=0, keepdims=True)
    lf_row = jnp.sum(jnp.where(sub == idx_f, gr, 0.0), axis=0, keepdims=True)

    ti = lax.broadcasted_iota(jnp.int32, (L, L), 0)
    si = lax.broadcasted_iota(jnp.int32, (L, L), 1)
    mask = (si >= ti) if reverse else (si <= ti)
    mask_t = (ti >= si) if reverse else (ti <= si)
    tri = jnp.where(mask, 1.0, 0.0).astype(BF16)
    tri_t = jnp.where(mask_t, 1.0, 0.0).astype(BF16)
    b_col = _split_dot(tri, jnp.broadcast_to(lf_col, (L, LANES)), split_lhs=False)[:, 0:1]
    b_row = _split_dot(jnp.broadcast_to(lf_row, (2 * SUBLANES, L)), tri_t, split_lhs=True)[0:1]

    d = jnp.where(mask, b_col - b_row + i_row, -jnp.inf)
    m_prev = m_ref[0:1, 0:1]
    g = b_col + m_prev
    mt = jnp.maximum(g, jnp.max(d, axis=1, keepdims=True))
    s = lax.dot_general(qb, kb, (((1,), (1,)), ((), ())), preferred_element_type=F32)
    s = s * jnp.exp(d - mt)
    inter = jnp.exp(g - mt)

    v_aug = jnp.concatenate([v_ref[...], jnp.ones((L, LANES), BF16)], axis=1)
    nd = (jnp.dot(s.astype(BF16), v_aug, preferred_element_type=F32)
          + inter * jnp.dot(qb, ct_ref[...].astype(BF16), preferred_element_type=F32))
    den = nd[:, dv:dv + 1]
    hout = nd[:, :dv] * (1.0 / jnp.maximum(jnp.abs(den), jnp.exp(-mt)))

    bl = b_col[0:1] if reverse else b_col[L - 1:L]
    wl = bl - b_col + i_col
    m_new = jnp.maximum(bl + m_prev, jnp.max(wl, axis=0, keepdims=True))
    a = jnp.exp(bl + m_prev - m_new)
    w = jnp.exp(wl - m_new)
    vw = (v_aug.astype(F32) * w).astype(BF16)
    ct_ref[...] = a * ct_ref[...] + lax.dot_general(
        kb, vw, (((0,), (0,)), ((), ())), preferred_element_type=F32)
    m_ref[...] = jnp.broadcast_to(m_new, m_ref.shape)

    if finish:
        hs = hf_ref[...] + hout
        y = hs * lax.rsqrt(jnp.mean(hs * hs, axis=-1, keepdims=True) + NORM_EPS) * hg_ref[...]
        out_ref[...] = (y * jax.nn.sigmoid(og_ref[...].astype(F32))).astype(out_ref.dtype)
    else:
        out_ref[...] = hout


def _mlstm_pass(proj, gates_c, gates_r, cos, sin, n_batch, n_lat, n_ctx,
                reverse, h_fwd=None, head_gain=None):
    t_all = proj.shape[0]
    heads = MLSTM_HEADS
    d = proj.shape[1] // 3
    dk, dv = d // 16, d // 8
    nlat, nctx = n_lat // CHUNK, n_ctx // CHUNK
    finish = h_fwd is not None

    def blk(b, c):
        cc = (nctx - 1 - c) if reverse else c
        lc = (nlat - 1 - (c - nctx)) if reverse else (c - nctx)
        return jnp.where(c < nctx, n_batch * nlat + b * nctx + cc, b * nlat + lc)

    def lat_blk(c):
        lc = (nlat - 1 - (c - nctx)) if reverse else (c - nctx)
        return jnp.clip(lc, 0, nlat - 1)

    in_specs = [
        pl.BlockSpec((CHUNK, dk), lambda b, h, c: (blk(b, c), h)),
        pl.BlockSpec((CHUNK, dk), lambda b, h, c: (blk(b, c), heads + h)),
        pl.BlockSpec((CHUNK, dv), lambda b, h, c: (blk(b, c), heads + h)),
        pl.BlockSpec((CHUNK, LANES), lambda b, h, c: (blk(b, c), 0)),
        pl.BlockSpec((4 * heads, CHUNK), lambda b, h, c: (0, blk(b, c))),
        pl.BlockSpec((CHUNK, dk // 2), lambda b, h, c: (lat_blk(c), 0)),
        pl.BlockSpec((CHUNK, dk // 2), lambda b, h, c: (lat_blk(c), 0)),
    ]
    args = [proj, proj, proj, gates_c, gates_r, cos, sin]
    if finish:
        in_specs += [
            pl.BlockSpec((CHUNK, dv), lambda b, h, c: (blk(b, c), h)),
            pl.BlockSpec((CHUNK, dv), lambda b, h, c: (blk(b, c), 2 * heads + h)),
            pl.BlockSpec((None, 1, dv), lambda b, h, c: (h, 0, 0)),
        ]
        args += [h_fwd, proj, head_gain.reshape(heads, 1, dv)]
    return pl.pallas_call(
        functools.partial(_mlstm_kernel, reverse=reverse, finish=finish, nctx=nctx,
                          heads=heads, dk=dk, dv=dv),
        grid=(n_batch, heads, nctx + nlat),
        in_specs=in_specs,
        out_specs=pl.BlockSpec((CHUNK, dv), lambda b, h, c: (blk(b, c), h)),
        out_shape=jax.ShapeDtypeStruct((t_all, heads * dv), BF16 if finish else F32),
        scratch_shapes=[pltpu.VMEM((dk, dv + LANES), F32), pltpu.VMEM((SUBLANES, LANES), F32)],
        compiler_params=_params(("arbitrary", "arbitrary", "arbitrary")),
        name="mlstm_bwd_finish" if finish else "mlstm_fwd",
    )(*args)


def _na_bias_blocks(rpb, rows):
    qr, span = NA_QROWS, 3 * NA_QROWS
    nh = rpb.shape[0]
    n_dr = 2 * NA_KH - 1
    c = np.arange(GRID_W)[:, None]
    cp = np.arange(GRID_W)[None, :]
    cs = np.clip(c - NA_KW // 2, 0, GRID_W - NA_KW)
    valid_c = (cp >= cs) & (cp < cs + NA_KW)
    onehot = ((cp - c + NA_KW - 1)[..., None] == np.arange(2 * NA_KW - 1)) & valid_c[..., None]
    colmat = jnp.einsum("hrd,cpd->hrcp", rpb, jnp.asarray(onehot, F32), precision=lax.Precision.HIGHEST)
    colmat = jnp.where(valid_c, colmat, -jnp.inf)
    masked = jnp.full((nh, GRID_W, GRID_W), -jnp.inf, F32)
    out = []
    for r0, u0 in ((0, 0), (qr, 0), (rows - qr, rows - span)):
        row_blocks = []
        for rl in range(qr):
            r = r0 + rl
            rs = min(max(r - NA_KH // 2, 0), rows - NA_KH)
            blocks = []
            for i in range(span):
                ia = u0 + i
                dr = ia - r + NA_KH - 1
                blocks.append(colmat[:, dr] if rs <= ia < rs + NA_KH and 0 <= dr < n_dr else masked)
            row_blocks.append(jnp.concatenate(blocks, axis=-1))
        out.append(jnp.concatenate(row_blocks, axis=-2))
    return jnp.stack(out, axis=0)


def _na_kernel(q_ref, k0_ref, k1_ref, k2_ref, v0_ref, v1_ref, v2_ref, kc_ref, vc_ref, bias_ref, o_ref):
    hd = NA_HEAD_DIM
    nt = (((1,), (1,)), ((), ()))
    ones = jnp.ones((CHUNK, hd), BF16)
    for hh in range(NA_HEAD_BLOCK):
        lanes = slice(hh * hd, (hh + 1) * hd)
        q = q_ref[:, lanes]
        scores = [lax.dot_general(q, k_ref[:, lanes], nt, preferred_element_type=F32)
                  + bias_ref[hh, :, j * CHUNK:(j + 1) * CHUNK]
                  for j, k_ref in enumerate((k0_ref, k1_ref, k2_ref))]
        scores.append(lax.dot_general(q, kc_ref[:, lanes], nt, preferred_element_type=F32))
        m = jnp.maximum(jnp.maximum(scores[0], scores[1]), jnp.maximum(scores[2], scores[3]))
        m = m.max(axis=-1, keepdims=True)
        acc = None
        for s, v_ref in zip(scores, (v0_ref, v1_ref, v2_ref, vc_ref)):
            v_aug = jnp.concatenate([v_ref[:, lanes], ones], axis=1)
            pv = jnp.dot(jnp.exp(s - m).astype(BF16), v_aug, preferred_element_type=F32)
            acc = pv if acc is None else acc + pv
        o_ref[:, lanes] = (acc[:, :hd] * (1.0 / acc[:, hd:hd + 1])).astype(o_ref.dtype)


def _na_attention(qkv, bias, n_batch, n_lat, n_ctx):
    d = qkv.shape[1] // 3
    hb = NA_HEAD_BLOCK
    width = hb * NA_HEAD_DIM
    nhb = d // width
    nrb = n_lat // CHUNK
    assert n_ctx == CHUNK and nrb >= 3

    def kv_map(off, j):
        return lambda h, b, r: (b * nrb + jnp.clip(r - 1, 0, nrb - 3) + j, off * nhb + h)

    def variant(r):
        return jnp.where(r == 0, 0, jnp.where(r == nrb - 1, 2, 1))

    blk = (CHUNK, width)
    in_specs = [pl.BlockSpec(blk, lambda h, b, r: (b * nrb + r, h))]
    in_specs += [pl.BlockSpec(blk, kv_map(1, j)) for j in range(3)]
    in_specs += [pl.BlockSpec(blk, kv_map(2, j)) for j in range(3)]
    in_specs += [
        pl.BlockSpec(blk, lambda h, b, r: (n_batch * nrb + b, nhb + h)),
        pl.BlockSpec(blk, lambda h, b, r: (n_batch * nrb + b, 2 * nhb + h)),
        pl.BlockSpec((None, hb, CHUNK, 3 * CHUNK), lambda h, b, r: (variant(r), h, 0, 0)),
    ]
    return pl.pallas_call(
        _na_kernel,
        grid=(nhb, n_batch, nrb),
        in_specs=in_specs,
        out_specs=pl.BlockSpec(blk, lambda h, b, r: (b * nrb + r, h)),
        out_shape=jax.ShapeDtypeStruct((n_batch * n_lat, d), BF16),
        compiler_params=_params(("arbitrary", "arbitrary", "arbitrary")),
        name="na_attention",
    )(*([qkv] * 9), bias)


def _sort16_pairs():
    pairs = []

    def merge(lo, n, r):
        step = r * 2
        if step < n:
            merge(lo, n, step)
            merge(lo + r, n, step)
            pairs.extend((i, i + r) for i in range(lo + r, lo + n - r, step))
        else:
            pairs.append((lo, lo + r))

    def sort(lo, n):
        if n > 1:
            sort(lo, n // 2)
            sort(lo + n // 2, n // 2)
            merge(lo, n, 1)

    sort(0, PEER_TOPK)
    return pairs


_SORT16 = _sort16_pairs()


def _bitonic_merge16(x):
    x = list(x)
    d = PEER_TOPK // 2
    while d:
        for i in range(PEER_TOPK):
            if i & d == 0:
                x[i], x[i + d] = jnp.maximum(x[i], x[i + d]), jnp.minimum(x[i], x[i + d])
        d //= 2
    return x


def _merge_top16(top, other):
    n = len(other)
    merged = list(top[:PEER_TOPK - n])
    merged += [jnp.maximum(top[i], other[PEER_TOPK - 1 - i]) for i in range(PEER_TOPK - n, PEER_TOPK)]
    return _bitonic_merge16(merged)


def _top16_sorted(x):
    rows = [x[SUBLANES * r:SUBLANES * (r + 1)] for r in range(PEER_TOPK)]
    for i, j in _SORT16:
        rows[i], rows[j] = jnp.maximum(rows[i], rows[j]), jnp.minimum(rows[i], rows[j])
    shift = SUBLANES // 2
    while shift:
        rolled = [pltpu.roll(v, shift, 0) for v in rows]
        rows = _merge_top16(rows, rolled)
        shift //= 2
    return rows


def _top16_pair_sums(a, b):
    top = [a[0] + b[j] for j in range(PEER_TOPK)]
    for i in range(1, SUBLANES):
        top = _merge_top16(top, [a[i] + b[j] for j in range(PEER_TOPK // (i + 1))])
    return _merge_top16(top, [a[i] + b[0] for i in range(SUBLANES, PEER_TOPK)])


def _peer_select_kernel(ht_ref, wq_ref, k1_ref, k2_ref, s1_ref, s2_ref, st_ref, qt_ref):
    heads = s1_ref.shape[0]
    tm = ht_ref.shape[1]
    dk = qt_ref.shape[0] // heads
    half = dk // 2
    nlb = tm // LANES
    r8 = lax.broadcasted_iota(jnp.int32, (SUBLANES, LANES), 0) % nlb
    qt_ref[...] = jnp.dot(wq_ref[...], ht_ref[...], preferred_element_type=F32)

    def head(h, carry):
        row = pl.multiple_of(h * dk, dk)
        q1 = qt_ref[pl.ds(row, half), :].astype(BF16)
        q2 = qt_ref[pl.ds(row + half, half), :].astype(BF16)
        s1 = jnp.dot(k1_ref[...], q1, preferred_element_type=F32)
        s2 = jnp.dot(k2_ref[...], q2, preferred_element_type=F32)
        tops1, tops2 = [], []
        for lb in range(nlb):
            sl = slice(lb * LANES, (lb + 1) * LANES)
            tops1.append(_top16_sorted(s1[:, sl]))
            tops2.append(_top16_sorted(s2[:, sl]))

        def pack(tops, r):
            out = tops[nlb - 1][r]
            for lb in range(nlb - 2, -1, -1):
                out = jnp.where(r8 == lb, tops[lb][r], out)
            return out

        a = [pack(tops1, r) for r in range(PEER_TOPK)]
        b = [pack(tops2, r) for r in range(PEER_TOPK)]
        a = [v - a[0] for v in a]
        b = [v - b[0] for v in b]
        top = _top16_pair_sums(a, b)
        z = jnp.exp(top[0])
        for v in top[1:]:
            z = z + jnp.exp(v)
        log_z = jnp.log(z)
        b = [v - log_z for v in b]
        st_ref[h] = _top16_pair_sums(a, b)[PEER_TOPK - 1]
        for lb in range(nlb):
            sl = slice(lb * LANES, (lb + 1) * LANES)
            s1_ref[h, :, sl] = s1[:, sl] - tops1[lb][0][0:1]
            s2_ref[h, :, sl] = (s2[:, sl] - tops2[lb][0][0:1]) - log_z[lb:lb + 1]
        return carry

    lax.fori_loop(0, heads, head, 0)


def _peer_select(ht, wq_t, keys1, keys2):
    d, t = ht.shape
    heads = PEER_HEADS
    dk = wq_t.shape[0] // heads
    nk = keys1.shape[0]
    tm = PEER_TM
    assert nk == PEER_TOPK * SUBLANES and tm // LANES <= SUBLANES
    out_s = jax.ShapeDtypeStruct((heads, nk, t), F32)
    return pl.pallas_call(
        _peer_select_kernel,
        grid=(t // tm,),
        in_specs=[
            pl.BlockSpec((d, tm), lambda i: (0, i)),
            pl.BlockSpec((heads * dk, d), lambda i: (0, 0)),
            pl.BlockSpec((nk, dk // 2), lambda i: (0, 0)),
            pl.BlockSpec((nk, dk // 2), lambda i: (0, 0)),
        ],
        out_specs=[
            pl.BlockSpec((heads, nk, tm), lambda i: (0, 0, i)),
            pl.BlockSpec((heads, nk, tm), lambda i: (0, 0, i)),
            pl.BlockSpec((heads, None, SUBLANES, LANES), lambda i: (0, i, 0, 0)),
        ],
        out_shape=[out_s, out_s, jax.ShapeDtypeStruct((heads, t // tm, SUBLANES, LANES), F32)],
        scratch_shapes=[pltpu.VMEM((heads * dk, tm), F32)],
        compiler_params=_params(("arbitrary",)),
        name="peer_select",
    )(ht, wq_t, keys1, keys2)


def _gelu(x):
    return 0.5 * x * (1.0 + lax.erf(x * np.float32(1.0 / np.sqrt(2.0))))


def _peer_dense_kernel(ht_ref, u_ref, vt_ref, s1_ref, s2_ref, st_ref, o_ref, hid0_ref, hid1_ref, act_ref):
    j = pl.program_id(1)
    heads, nk, tm = s2_ref.shape
    te = u_ref.shape[0]
    groups = te // nk
    assert 2 * groups == s1_ref.shape[1]

    last = pl.num_programs(1) - 1

    def up_project(hid_w_ref):
        hid_w_ref[...] = jnp.dot(u_ref[...], ht_ref[...], preferred_element_type=F32)

    def step(hid_w_ref, hid_r_ref, k1_base):
        if hid_w_ref is not None:
            up_project(hid_w_ref)
        for lb in range(tm // LANES):
            sl = slice(lb * LANES, (lb + 1) * LANES)
            for kk in range(groups):
                k1 = k1_base + kk
                for r0 in range(0, nk, PEER_SUB):
                    gate = jnp.zeros((PEER_SUB, LANES), F32)
                    for h in range(heads):
                        s = s1_ref[h, k1:k1 + 1, sl] + s2_ref[h, r0:r0 + PEER_SUB, sl]
                        gate = gate + jnp.where(s >= st_ref[h, lb:lb + 1, :], jnp.exp(s), 0.0)
                    rows = slice(kk * nk + r0, kk * nk + r0 + PEER_SUB)
                    act_ref[rows, sl] = (gate * _gelu(hid_r_ref[rows, sl])).astype(BF16)
        o_ref[...] += jnp.dot(vt_ref[...], act_ref[...], preferred_element_type=F32)

    @pl.when(j == 0)
    def _():
        o_ref[...] = jnp.zeros_like(o_ref)
        up_project(hid0_ref)

    @pl.when((j % 2 == 0) & (j > 0) & (j < last))
    def _():
        step(hid0_ref, hid1_ref, groups)

    @pl.when(j % 2 == 1)
    def _():
        step(hid1_ref, hid0_ref, 0)

    @pl.when(j == last)
    def _():
        step(None, hid1_ref, groups)


def _peer_dense(ht, u, vt, s1, s2, stats):
    d, t = ht.shape
    n_exp = u.shape[0]
    heads, nk, _ = s1.shape
    tm, te = PEER_TM, PEER_TE
    nblk = n_exp // te
    assert nblk % 2 == 0
    return pl.pallas_call(
        _peer_dense_kernel,
        grid=(t // tm, nblk + 1),
        in_specs=[
            pl.BlockSpec((d, tm), lambda i, j: (0, i)),
            pl.BlockSpec((te, d), lambda i, j: (jnp.minimum(j, nblk - 1), 0)),
            pl.BlockSpec((None, d, te), lambda i, j: (jnp.maximum(j - 1, 0), 0, 0)),
            pl.BlockSpec((heads, SUBLANES, tm), lambda i, j: (0, jnp.maximum(j - 1, 0) // 2, i)),
            pl.BlockSpec((heads, nk, tm), lambda i, j: (0, 0, i)),
            pl.BlockSpec((heads, None, SUBLANES, LANES), lambda i, j: (0, i, 0, 0)),
        ],
        out_specs=pl.BlockSpec((d, tm), lambda i, j: (0, i)),
        out_shape=jax.ShapeDtypeStruct((d, t), F32),
        scratch_shapes=[pltpu.VMEM((te, tm), F32), pltpu.VMEM((te, tm), F32), pltpu.VMEM((te, tm), BF16)],
        compiler_params=_params(("arbitrary", "arbitrary")),
        name="peer_dense",
    )(ht, u, vt, s1, s2, stats)


def _resid_t_kernel(x_ref, yt_ref, g_ref, o_ref):
    o_ref[...] = x_ref[...] + g_ref[...] * yt_ref[...].T


def _resid_t(x, yt, gate, tiles_per_batch, n_batch):
    d, t = yt.shape
    tm = ROW_TILE
    return pl.pallas_call(
        _resid_t_kernel,
        grid=(t // tm,),
        in_specs=[
            pl.BlockSpec((tm, d), lambda i: (i, 0)),
            pl.BlockSpec((d, tm), lambda i: (0, i)),
            pl.BlockSpec((None, 1, d), _mod_row_map(tiles_per_batch, n_batch)),
        ],
        out_specs=pl.BlockSpec((tm, d), lambda i: (i, 0)),
        out_shape=jax.ShapeDtypeStruct((t, d), F32),
        compiler_params=_params(("arbitrary",)),
        name="resid_t",
    )(x, yt, gate)


def _peer_ffn(x, t_rows, gain, mods, w_query, sub_keys, u, v, n_lat, n_batch):
    shift, scale, gate = mods
    ht = _normmod(x, t_rows, gain, shift, scale, n_lat // ROW_TILE, n_batch, transpose=True)
    s1, s2, stats = _peer_select(ht, w_query.T.astype(BF16), sub_keys[0].astype(BF16),
                                 sub_keys[1].astype(BF16))
    d = v.shape[1]
    vt = v.astype(BF16).reshape(-1, PEER_TE, d).transpose(0, 2, 1)
    yt = _peer_dense(ht, u.astype(BF16), vt, s1, s2, stats)
    return _resid_t(x, yt, gate, n_lat // ROW_TILE, n_batch)


def _axial_rope(n, head_dim):
    quarter = head_dim // 4
    inv = ROPE_THETA ** (-jnp.arange(quarter, dtype=F32) / quarter)
    t = jnp.arange(n, dtype=jnp.int32)
    row = (t // GRID_W).astype(F32)
    col = (t % GRID_W).astype(F32)
    ang = jnp.concatenate([row[:, None] * inv, col[:, None] * inv], axis=-1)
    return jnp.cos(ang), jnp.sin(ang)


def kernel(x, c, ctx, c_ctx, ada_w, ada_b, norm_mix, norm_ffn, mlstm_w_in, mlstm_gate_b, mlstm_head_gain, mlstm_w_out, na_w_qkv, na_q_gain, na_k_gain, na_rpb, na_w_out, peer_w_query, peer_sub_keys, peer_u, peer_v):
    n_batch, n_lat, d = x.shape
    n_ctx = ctx.shape[1]
    depth = ada_w.shape[0]
    t_lat = n_batch * n_lat
    t_all = t_lat + n_batch * n_ctx
    heads = MLSTM_HEADS
    assert depth == 2 and n_lat % MM_TM == 0 and t_all % MM_TM == 0 and n_ctx % CHUNK == 0

    mod_rows = 2 * SUBLANES
    cvec = jnp.concatenate(
        [c, c_ctx[None, :], jnp.zeros((mod_rows - n_batch - 1, d), F32)], axis=0)
    mod = _ada(cvec, ada_w, ada_b).reshape(depth, mod_rows, N_ADA, d)

    def mods(layer, k):
        return mod[layer, :, k, :].reshape(mod_rows, 1, d)

    xs = (x.reshape(t_lat, d), ctx.reshape(n_batch * n_ctx, d))

    hmix = _normmod(xs, t_all, norm_mix[0], mods(0, 0), mods(0, 1), n_lat // ROW_TILE, n_batch,
                    transpose=False)
    w_in = mlstm_w_in[0]
    proj = _matmul(hmix, w_in[:, :3 * d].astype(BF16), BF16)
    w_gate = jnp.pad(w_in[:, 3 * d:], ((0, 0), (0, LANES - 4 * heads))).astype(BF16)
    gate_b = jnp.pad(mlstm_gate_b[0], (0, LANES - 4 * heads)).reshape(1, LANES)
    gates_c = _matmul_gates(hmix, w_gate, gate_b, heads)
    gates_r = gates_c[:, :4 * heads].T
    cos, sin = _axial_rope(n_lat, d // 16)
    scan_args = (proj, gates_c, gates_r, cos, sin, n_batch, n_lat, n_ctx)
    h_fwd = _mlstm_pass(*scan_args, reverse=False)
    ymix = _mlstm_pass(*scan_args, reverse=True, h_fwd=h_fwd, head_gain=mlstm_head_gain[0])
    xs = _matmul_resid(ymix, mlstm_w_out[0].astype(BF16), xs, mods(0, 2), n_lat // MM_TM, n_batch)
    xs = _peer_ffn(xs, t_all, norm_ffn[0], (mods(0, 3), mods(0, 4), mods(0, 5)),
                   peer_w_query[0], peer_sub_keys[0], peer_u[0], peer_v[0], n_lat, n_batch)

    hmix = _normmod(xs, t_all, norm_mix[1], mods(1, 0), mods(1, 1), n_lat // ROW_TILE, n_batch,
                    transpose=False)
    qkv = _matmul_qkv(hmix, na_w_qkv[0].astype(BF16), na_q_gain[0], na_k_gain[0], NA_HEAD_DIM)
    bias = _na_bias_blocks(na_rpb[0], n_lat // GRID_W)
    att = _na_attention(qkv, bias, n_batch, n_lat, n_ctx)
    xs = _matmul_resid(att, na_w_out[0].astype(BF16), xs, mods(1, 2), n_lat // MM_TM, n_batch)
    xs = _peer_ffn(xs, t_lat, norm_ffn[1], (mods(1, 3), mods(1, 4), mods(1, 5)),
                   peer_w_query[1], peer_sub_keys[1], peer_u[1], peer_v[1], n_lat, n_batch)
    return xs.reshape(n_batch, n_lat, d)
```
